```python
import jax, jax.numpy as jnp
from jax import lax
import numpy as np

D_MODEL = 2048
BATCH = 4
SEQ = 2048
DEPTH = 4

HEAD_DIM = 128
D_ATT = D_MODEL // 2
N_HEADS_ATT = D_ATT // HEAD_DIM
Q_BLOCK = 128
D_SC = D_MODEL // 4
N_GROUPS_SC = 4
SC_WIDTH = 3
D_CF = D_MODEL // 4
N_GROUPS_CF = 4
CF_WIDTH = 31
N_BRANCH = 3
D_FF = 4 * D_MODEL
N_IN = 3 * D_ATT + 3 * D_SC + 2 * D_CF + N_BRANCH * D_MODEL
RMS_EPS = 1e-6
LN_EPS = 1e-5

kernel_name = 'gated_parallel_hybrid_sb_conv_block'


def rms_norm(x, g):
    xf = x.astype(jnp.float32)
    y = xf * lax.rsqrt(jnp.mean(jnp.square(xf), axis=-1, keepdims=True) + RMS_EPS)
    return (y * g.astype(jnp.float32)).astype(x.dtype)


def layer_norm(x, g, b):
    xf = x.astype(jnp.float32)
    mu = jnp.mean(xf, axis=-1, keepdims=True)
    var = jnp.mean(jnp.square(xf - mu), axis=-1, keepdims=True)
    y = (xf - mu) * lax.rsqrt(var + LN_EPS)
    return (y * g.astype(jnp.float32) + b.astype(jnp.float32)).astype(x.dtype)


def causal_dwconv(u, w):
    K, C = w.shape
    return lax.conv_general_dilated(
        u, w[:, None, :].astype(u.dtype), window_strides=(1,), padding=[(K - 1, 0)],
        dimension_numbers=('NWC', 'WIO', 'NWC'), feature_group_count=C)


def stick_breaking_attention(q, k, v):
    S = q.shape[1]
    scale = HEAD_DIM ** -0.5
    outs = []
    for blk in range(S // Q_BLOCK):
        q0 = blk * Q_BLOCK
        kend = q0 + Q_BLOCK
        qb = q[:, q0:kend]
        kb = k[:, :kend]
        vb = v[:, :kend]
        z = jnp.einsum('bthd,bshd->bhts', qb, kb).astype(jnp.float32) * scale
        t_idx = q0 + jnp.arange(Q_BLOCK)[:, None]
        s_idx = jnp.arange(kend)[None, :]
        mask = s_idx < t_idx
        log_fail = jnp.where(mask, jax.nn.log_sigmoid(-z), 0.0)
        suffix = lax.cumsum(log_fail, axis=3, reverse=True) - log_fail
        log_a = jax.nn.log_sigmoid(z) + suffix
        a = jnp.where(mask, jnp.exp(log_a), 0.0)
        outs.append(jnp.einsum('bhts,bshd->bthd', a.astype(v.dtype), vb))
    return jnp.concatenate(outs, axis=1)


def setup_inputs(seed: int = 0) -> dict:
    key = jax.random.key(seed)
    ks = jax.random.split(key, 20)
    f32 = jnp.float32

    def nrm(k, shape, fan_in):
        return jax.random.normal(k, shape, f32) * (fan_in ** -0.5)

    def gain(k, shape):
        return jnp.ones(shape, f32) + 0.02 * jax.random.normal(k, shape, f32)

    return {
        'x': jax.random.normal(ks[0], (BATCH, SEQ, D_MODEL), f32),
        'ln_mix_pre': gain(ks[1], (DEPTH, D_MODEL)),
        'ln_mix_post': gain(ks[2], (DEPTH, D_MODEL)),
        'ln_mlp_pre': gain(ks[3], (DEPTH, D_MODEL)),
        'ln_mlp_post': gain(ks[4], (DEPTH, D_MODEL)),
        'w_in': nrm(ks[5], (DEPTH, D_MODEL, N_IN), D_MODEL),
        'conv_a_w': nrm(ks[6], (DEPTH, SC_WIDTH, D_SC), SC_WIDTH),
        'proj_a': nrm(ks[7], (DEPTH, D_SC, D_MODEL), D_SC),
        'proj_b': nrm(ks[8], (DEPTH, D_ATT, D_MODEL), D_ATT),
        'conv_c_w': nrm(ks[9], (DEPTH, CF_WIDTH, D_CF), CF_WIDTH),
        'conv_c_b': 0.02 * jax.random.normal(ks[10], (DEPTH, D_CF), f32),
        'norm_c_g': gain(ks[11], (DEPTH, D_CF)),
        'norm_c_b': 0.02 * jax.random.normal(ks[12], (DEPTH, D_CF), f32),
        'proj_c': nrm(ks[13], (DEPTH, D_CF, D_MODEL), D_CF),
        'w_o': nrm(ks[14], (DEPTH, D_MODEL, D_MODEL), D_MODEL),
        'w_up': nrm(ks[15], (DEPTH, D_MODEL, D_FF), D_MODEL),
        'w_down': nrm(ks[16], (DEPTH, D_FF, D_MODEL), D_FF),
    }


def reference(x, ln_mix_pre, ln_mix_post, ln_mlp_pre, ln_mlp_post, w_in, conv_a_w, proj_a,
              proj_b, conv_c_w, conv_c_b, norm_c_g, norm_c_b, proj_c, w_o, w_up, w_down):
    Bsz, S, _ = x.shape
    sizes = [D_ATT, D_ATT, D_ATT, D_SC, D_SC, D_SC, 2 * D_CF, N_BRANCH * D_MODEL]
    splits = [int(c) for c in np.cumsum(sizes)[:-1]]
    for l in range(DEPTH):
        h = rms_norm(x, ln_mix_pre[l])
        proj = h @ w_in[l]
        q, k, v, sc_b, sc_c, sc_u, cf_in, gate_logits = jnp.split(proj, splits, axis=-1)
        ya = (sc_b * causal_dwconv(sc_c * sc_u, conv_a_w[l])) @ proj_a[l]
        q = q.reshape(Bsz, S, N_HEADS_ATT, HEAD_DIM)
        k = k.reshape(Bsz, S, N_HEADS_ATT, HEAD_DIM)
        v = v.reshape(Bsz, S, N_HEADS_ATT, HEAD_DIM)
        yb = stick_breaking_attention(q, k, v).reshape(Bsz, S, D_ATT) @ proj_b[l]
        cf_a, cf_g = jnp.split(cf_in, 2, axis=-1)
        u = cf_a * jax.nn.sigmoid(cf_g)
        u = causal_dwconv(u, conv_c_w[l]) + conv_c_b[l]
        u = jax.nn.silu(layer_norm(u, norm_c_g[l], norm_c_b[l]))
        yc = u @ proj_c[l]
        g_a, g_b, g_c = jnp.split(jax.nn.sigmoid(gate_logits), N_BRANCH, axis=-1)
        mixed = (g_a * ya + g_b * yb + g_c * yc) @ w_o[l]
        x = x + rms_norm(mixed, ln_mix_post[l])
        h = rms_norm(x, ln_mlp_pre[l])
        f = jnp.square(jax.nn.relu(h @ w_up[l])) @ w_down[l]
        x = x + rms_norm(f, ln_mlp_post[l])
    return x
```

```python
import functools

import jax
import jax.numpy as jnp
from jax import lax
from jax.experimental import pallas as pl
from jax.experimental.pallas import tpu as pltpu

F32 = jnp.float32
BF16 = jnp.bfloat16

D_MODEL = 2048
HEAD_DIM = 128
D_ATT = D_MODEL // 2
N_HEADS = D_ATT // HEAD_DIM
D_SC = D_MODEL // 4
SC_WIDTH = 3
D_CF = D_MODEL // 4
CF_WIDTH = 31
D_FF = 4 * D_MODEL
N_IN = 3 * D_ATT + 3 * D_SC + 2 * D_CF + 3 * D_MODEL
RMS_EPS = 1e-6
LN_EPS = 1e-5

COL_GATE = 0
COL_Q = 3 * D_MODEL
COL_K = COL_Q + D_ATT
COL_V = COL_K + D_ATT
COL_CF = COL_V + D_ATT
COL_SC = COL_CF + 2 * D_CF
assert COL_SC + 3 * D_SC == N_IN

LANES = 128
MIB = 1024 * 1024

NORM_CHUNK = 32
IN_TM, IN_TN = 1024, 512
ATT_TQ = 256
ATT_HEADS_PER_STEP = 2
CONV_TR = 256
CONV_HALO = 32
CONV_CHUNK = 32
MERGE_TM = 256
MLP_TM, MLP_TF = 512, 1024


def _params(semantics, vmem_mib):
    return pltpu.CompilerParams(dimension_semantics=semantics,
                                vmem_limit_bytes=vmem_mib * MIB)


def _rmsnorm_rows(x_ref, g_ref, h_ref):
    g = g_ref[...]

    def body(c, carry):
        r = pl.multiple_of(c * NORM_CHUNK, NORM_CHUNK)
        x = x_ref[pl.ds(r, NORM_CHUNK), :]
        ms = jnp.mean(x * x, axis=-1, keepdims=True)
        h_ref[pl.ds(r, NORM_CHUNK), :] = (x * lax.rsqrt(ms + RMS_EPS) * g).astype(h_ref.dtype)
        return carry

    lax.fori_loop(0, x_ref.shape[0] // NORM_CHUNK, body, 0)


def _in_proj_kernel(x_ref, g_ref, w_ref, o_ref, h_ref):
    @pl.when(pl.program_id(1) == 0)
    def _():
        _rmsnorm_rows(x_ref, g_ref, h_ref)

    o_ref[...] = jnp.dot(h_ref[...], w_ref[...], preferred_element_type=F32).astype(o_ref.dtype)


def _in_proj(x, g, w):
    t = x.shape[0]
    return pl.pallas_call(
        _in_proj_kernel,
        grid=(t // IN_TM, N_IN // IN_TN),
        in_specs=[
            pl.BlockSpec((IN_TM, D_MODEL), lambda i, j: (i, 0)),
            pl.BlockSpec((1, D_MODEL), lambda i, j: (0, 0)),
            pl.BlockSpec((D_MODEL, IN_TN), lambda i, j: (0, j)),
        ],
        out_specs=pl.BlockSpec((IN_TM, IN_TN), lambda i, j: (i, j)),
        out_shape=jax.ShapeDtypeStruct((t, N_IN), BF16),
        scratch_shapes=[pltpu.VMEM((IN_TM, D_MODEL), BF16)],
        compiler_params=_params(("parallel", "arbitrary"), 48),
        name="in_proj",
    )(x, g, w)


def _sb_block(q, k_blk, v_blk, uo, carry, mask):
    scale = HEAD_DIM ** -0.5
    z = lax.dot_general(q, k_blk, (((1,), (1,)), ((), ())), preferred_element_type=F32) * scale
    log_fail = -(jnp.maximum(z, 0.0) + jnp.log1p(jnp.exp(-jnp.abs(z))))
    if mask is not None:
        log_fail = jnp.where(mask, log_fail, 0.0)
    log_beta = z + log_fail
    a_parts = [None] * (ATT_TQ // LANES)
    for c in reversed(range(ATT_TQ // LANES)):
        sl = slice(c * LANES, (c + 1) * LANES)
        lf = log_fail[:, sl]
        hi = lf.astype(BF16)
        lo = (lf - hi.astype(F32)).astype(BF16)
        st = jnp.dot(jnp.concatenate([hi, lo], axis=1), uo, preferred_element_type=F32)
        suffix = st[:, :LANES] + carry
        a = jnp.exp(log_beta[:, sl] + suffix)
        if mask is not None:
            a = jnp.where(mask[:, sl], a, 0.0)
        a_parts[c] = a.astype(BF16)
        carry = carry + st[:, LANES:]
    pv = jnp.dot(jnp.concatenate(a_parts, axis=1), v_blk, preferred_element_type=F32)
    return carry, pv


def _attn_kernel(q_ref, k_ref, v_ref, uo_ref, o_ref):
    qi = pl.program_id(2)
    uo = uo_ref[...]
    row = lax.broadcasted_iota(jnp.int32, (ATT_TQ, ATT_TQ), 0)
    col = lax.broadcasted_iota(jnp.int32, (ATT_TQ, ATT_TQ), 1)
    diag_mask = col < row
    heads = [slice(h * HEAD_DIM, (h + 1) * HEAD_DIM) for h in range(ATT_HEADS_PER_STEP)]
    qs = [q_ref[:, hs] for hs in heads]

    def pair(ks, state, mask):
        out = []
        for hs, q, (carry, acc) in zip(heads, qs, state):
            carry, pv = _sb_block(q, k_ref[pl.ds(ks, ATT_TQ), hs], v_ref[pl.ds(ks, ATT_TQ), hs],
                                  uo, carry, mask)
            out.append((carry, acc + pv))
        return tuple(out)

    zeros = jnp.zeros((ATT_TQ, HEAD_DIM), F32)
    state = tuple((zeros, zeros) for _ in heads)
    state = pair(pl.multiple_of(qi * ATT_TQ, ATT_TQ), state, diag_mask)

    def body(i, state):
        return pair(pl.multiple_of((qi - 1 - i) * ATT_TQ, ATT_TQ), state, None)

    state = lax.fori_loop(0, qi, body, state)
    for hs, (_, acc) in zip(heads, state):
        o_ref[:, hs] = acc.astype(o_ref.dtype)


def _attention(proj, uo, batch, seq):
    nq = seq // ATT_TQ
    width = ATT_HEADS_PER_STEP * HEAD_DIM
    return pl.pallas_call(
        _attn_kernel,
        grid=(batch, N_HEADS // ATT_HEADS_PER_STEP, nq),
        in_specs=[
            pl.BlockSpec((ATT_TQ, width), lambda b, h, i: (b * nq + i, COL_Q // width + h)),
            pl.BlockSpec((seq, width), lambda b, h, i: (b, COL_K // width + h)),
            pl.BlockSpec((seq, width), lambda b, h, i: (b, COL_V // width + h)),
            pl.BlockSpec((2 * LANES, 2 * LANES), lambda b, h, i: (0, 0)),
        ],
        out_specs=pl.BlockSpec((ATT_TQ, width), lambda b, h, i: (b * nq + i, h)),
        out_shape=jax.ShapeDtypeStruct((batch * seq, D_ATT), BF16),
        compiler_params=_params(("parallel", "parallel", "arbitrary"), 32),
        name="sb_attn",
    )(proj, proj, proj, uo)


def _conv_kernel(scb_ref, scc_ref, scu_ref, scc_h_ref, scu_h_ref, cf_ref, cf_h_ref,
                 wa_ref, wc_ref, bc_ref, lng_ref, lnb_ref, a_out_ref, c_out_ref, p_scr, u_scr):
    first = pl.program_id(1) == 0

    p_halo = scc_h_ref[...].astype(F32) * scu_h_ref[...].astype(F32)
    p_scr[:CONV_HALO, :] = jnp.where(first, 0.0, p_halo)
    p_scr[CONV_HALO:, :] = scc_ref[...].astype(F32) * scu_ref[...].astype(F32)

    def glu(ref):
        v = ref[...].astype(F32)
        return v[:, :D_CF] * jax.nn.sigmoid(v[:, D_CF:])

    u_scr[:CONV_HALO, :] = jnp.where(first, 0.0, glu(cf_h_ref))
    u_scr[CONV_HALO:, :] = glu(cf_ref)

    wa = wa_ref[...]
    wc = wc_ref[...]
    for c in range(CONV_TR // CONV_CHUNK):
        r0 = c * CONV_CHUNK
        rows = slice(r0, r0 + CONV_CHUNK)
        base = CONV_HALO - (SC_WIDTH - 1) + r0
        acc = wa[0:1, :] * p_scr[base:base + CONV_CHUNK, :]
        for k in range(1, SC_WIDTH):
            acc = acc + wa[k:k + 1, :] * p_scr[base + k:base + k + CONV_CHUNK, :]
        a_out_ref[rows, :] = (scb_ref[rows, :].astype(F32) * acc).astype(a_out_ref.dtype)
        base = CONV_HALO - (CF_WIDTH - 1) + r0
        acc = wc[0:1, :] * u_scr[base:base + CONV_CHUNK, :]
        for k in range(1, CF_WIDTH):
            acc = acc + wc[k:k + 1, :] * u_scr[base + k:base + k + CONV_CHUNK, :]
        acc = acc + bc_ref[...]
        mu = jnp.mean(acc, axis=-1, keepdims=True)
        cen = acc - mu
        var = jnp.mean(cen * cen, axis=-1, keepdims=True)
        y = cen * lax.rsqrt(var + LN_EPS) * lng_ref[...] + lnb_ref[...]
        c_out_ref[rows, :] = (y * jax.nn.sigmoid(y)).astype(c_out_ref.dtype)


def _conv_branches(proj, wa, wc, bc, lng, lnb, batch, seq):
    nr = seq // CONV_TR
    hr = CONV_TR // CONV_HALO

    def cur(width, col):
        return pl.BlockSpec((CONV_TR, width), lambda b, i: (b * nr + i, col // width))

    def halo(width, col):
        return pl.BlockSpec((CONV_HALO, width),
                            lambda b, i: (jnp.maximum((b * nr + i) * hr - 1, 0), col // width))

    def whole(shape):
        return pl.BlockSpec(shape, lambda b, i: (0, 0))

    return pl.pallas_call(
        _conv_kernel,
        grid=(batch, nr),
        in_specs=[
            cur(D_SC, COL_SC), cur(D_SC, COL_SC + D_SC), cur(D_SC, COL_SC + 2 * D_SC),
            halo(D_SC, COL_SC + D_SC), halo(D_SC, COL_SC + 2 * D_SC),
            cur(2 * D_CF, COL_CF), halo(2 * D_CF, COL_CF),
            whole((SC_WIDTH, D_SC)), whole((CF_WIDTH, D_CF)),
            whole((1, D_CF)), whole((1, D_CF)), whole((1, D_CF)),
        ],
        out_specs=[
            pl.BlockSpec((CONV_TR, D_SC), lambda b, i: (b * nr + i, 0)),
            pl.BlockSpec((CONV_TR, D_CF), lambda b, i: (b * nr + i, 0)),
        ],
        out_shape=[
            jax.ShapeDtypeStruct((batch * seq, D_SC), BF16),
            jax.ShapeDtypeStruct((batch * seq, D_CF), BF16),
        ],
        scratch_shapes=[
            pltpu.VMEM((CONV_HALO + CONV_TR, D_SC), F32),
            pltpu.VMEM((CONV_HALO + CONV_TR, D_CF), F32),
        ],
        compiler_params=_params(("parallel", "arbitrary"), 32),
        name="conv_branches",
    )(proj, proj, proj, proj, proj, proj, proj, wa, wc, bc, lng, lnb)


def _merge_kernel(a_ref, att_ref, c_ref, ga_ref, gb_ref, gc_ref, x_ref,
                  pa_ref, pb_ref, pc_ref, wo_ref, g_ref, o_ref):
    def gated(gate_ref, feat_ref, w_ref):
        y = jnp.dot(feat_ref[...], w_ref[...], preferred_element_type=F32)
        return jax.nn.sigmoid(gate_ref[...].astype(F32)) * y

    mixed = gated(ga_ref, a_ref, pa_ref) + gated(gb_ref, att_ref, pb_ref) + gated(gc_ref, c_ref, pc_ref)
    m = jnp.dot(mixed.astype(BF16), wo_ref[...], preferred_element_type=F32)
    ms = jnp.mean(m * m, axis=-1, keepdims=True)
    o_ref[...] = x_ref[...] + m * lax.rsqrt(ms + RMS_EPS) * g_ref[...]


def _merge(a_in, att, c_in, proj, x, pa, pb, pc, wo, g):
    t = x.shape[0]

    def rows(width, col=0):
        return pl.BlockSpec((MERGE_TM, width), lambda i: (i, col // width))

    def whole(shape):
        return pl.BlockSpec(shape, lambda i: (0, 0))

    return pl.pallas_call(
        _merge_kernel,
        grid=(t // MERGE_TM,),
        in_specs=[
            rows(D_SC), rows(D_ATT), rows(D_CF),
            rows(D_MODEL, COL_GATE), rows(D_MODEL, COL_GATE + D_MODEL), rows(D_MODEL, COL_GATE + 2 * D_MODEL),
            rows(D_MODEL),
            whole((D_SC, D_MODEL)), whole((D_ATT, D_MODEL)), whole((D_CF, D_MODEL)),
            whole((D_MODEL, D_MODEL)), whole((1, D_MODEL)),
        ],
        out_specs=rows(D_MODEL),
        out_shape=jax.ShapeDtypeStruct((t, D_MODEL), F32),
        compiler_params=_params(("parallel",), 56),
        name="merge",
    )(a_in, att, c_in, proj, proj, proj, x, pa, pb, pc, wo, g)


def _mlp_kernel(x_ref, gpre_ref, wu_ref, wd_ref, gpost_ref, o_ref, h_ref, acc_ref):
    f = pl.program_id(1)

    @pl.when(f == 0)
    def _():
        _rmsnorm_rows(x_ref, gpre_ref, h_ref)

    u = jnp.dot(h_ref[...], wu_ref[...], preferred_element_type=F32)
    r = jnp.square(jnp.maximum(u, 0.0)).astype(BF16)
    part = jnp.dot(r, wd_ref[...], preferred_element_type=F32)

    @pl.when(f == 0)
    def _():
        acc_ref[...] = part

    @pl.when(f != 0)
    def _():
        acc_ref[...] += part

    @pl.when(f == pl.num_programs(1) - 1)
    def _():
        y = acc_ref[...]
        ms = jnp.mean(y * y, axis=-1, keepdims=True)
        o_ref[...] = x_ref[...] + y * lax.rsqrt(ms + RMS_EPS) * gpost_ref[...]


def _mlp(x, gpre, wu, wd, gpost):
    t = x.shape[0]
    return pl.pallas_call(
        _mlp_kernel,
        grid=(t // MLP_TM, D_FF // MLP_TF),
        in_specs=[
            pl.BlockSpec((MLP_TM, D_MODEL), lambda i, f: (i, 0)),
            pl.BlockSpec((1, D_MODEL), lambda i, f: (0, 0)),
            pl.BlockSpec((D_MODEL, MLP_TF), lambda i, f: (0, f)),
            pl.BlockSpec((MLP_TF, D_MODEL), lambda i, f: (f, 0)),
            pl.BlockSpec((1, D_MODEL), lambda i, f: (0, 0)),
        ],
        out_specs=pl.BlockSpec((MLP_TM, D_MODEL), lambda i, f: (i, 0)),
        out_shape=jax.ShapeDtypeStruct((t, D_MODEL), F32),
        scratch_shapes=[pltpu.VMEM((MLP_TM, D_MODEL), BF16), pltpu.VMEM((MLP_TM, D_MODEL), F32)],
        compiler_params=_params(("parallel", "arbitrary"), 48),
        name="mlp",
    )(x, gpre, wu, wd, gpost)


def _scan_matrix():
    j = jnp.arange(LANES)[:, None]
    s = jnp.arange(LANES)[None, :]
    half = jnp.concatenate([(j > s).astype(BF16), jnp.ones((LANES, LANES), BF16)], axis=1)
    return jnp.concatenate([half, half], axis=0)


def kernel(x, ln_mix_pre, ln_mix_post, ln_mlp_pre, ln_mlp_post, w_in, conv_a_w, proj_a, proj_b,
           conv_c_w, conv_c_b, norm_c_g, norm_c_b, proj_c, w_o, w_up, w_down):
    batch, seq, d = x.shape
    depth = w_in.shape[0]
    assert d == D_MODEL and w_in.shape[2] == N_IN
    assert seq % ATT_TQ == 0 and seq % CONV_TR == 0 and (batch * seq) % IN_TM == 0
    uo = _scan_matrix()
    q_end, sc_end, cf_end = 3 * D_ATT, 3 * D_ATT + 3 * D_SC, 3 * D_ATT + 3 * D_SC + 2 * D_CF
    xt = x.reshape(batch * seq, d)
    for l in range(depth):
        w = w_in[l]
        w_l = jnp.concatenate([w[:, cf_end:], w[:, :q_end], w[:, sc_end:cf_end], w[:, q_end:sc_end]],
                              axis=1).astype(BF16)
        proj = _in_proj(xt, ln_mix_pre[l][None, :], w_l)
        att = _attention(proj, uo, batch, seq)
        a_in, c_in = _conv_branches(proj, conv_a_w[l], conv_c_w[l], conv_c_b[l][None, :],
                                    norm_c_g[l][None, :], norm_c_b[l][None, :], batch, seq)
        xt = _merge(a_in, att, c_in, proj, xt,
                    proj_a[l].astype(BF16), proj_b[l].astype(BF16), proj_c[l].astype(BF16),
                    w_o[l].astype(BF16), ln_mix_post[l][None, :])
        xt = _mlp(xt, ln_mlp_pre[l][None, :], w_up[l].astype(BF16), w_down[l].astype(BF16),
                  ln_mlp_post[l][None, :])
    return xt.reshape(batch, seq, d)
```

```python
import math

import jax
import jax.numpy as jnp
from jax import lax
from jax.experimental import pallas as pl
from jax.experimental.pallas import tpu as pltpu

F32 = jnp.float32
BF16 = jnp.bfloat16

D_MODEL = 2048
HEAD_DIM = 128
D_ATT = D_MODEL // 2
N_HEADS = D_ATT // HEAD_DIM
D_SC = D_MODEL // 4
SC_WIDTH = 3
D_CF = D_MODEL // 4
CF_WIDTH = 31
D_FF = 4 * D_MODEL
N_IN = 3 * D_ATT + 3 * D_SC + 2 * D_CF + 3 * D_MODEL
RMS_EPS = 1e-6
LN_EPS = 1e-5

SRC_SC = 3 * D_ATT
SRC_CF = SRC_SC + 3 * D_SC
SRC_GATE = SRC_CF + 2 * D_CF
COL_GATE = 0
COL_Q = 3 * D_MODEL
COL_K = COL_Q + D_ATT
COL_V = COL_K + D_ATT
COL_CF = COL_V + D_ATT
COL_SC = COL_CF + 2 * D_CF
assert COL_SC + 3 * D_SC == N_IN

LANES = 128
MIB = 1024 * 1024

NORM_CHUNK = 16
NORM_UNROLL = 4
IN_TM, IN_TN = 2048, 512
ATT_TQ = 256
ATT_HEADS_PER_STEP = 8
ATT_LOG2_SCALE = HEAD_DIM ** -0.5 * math.log2(math.e)
LOG2_E = math.log2(math.e)
MASKED_LOG2 = -1e30
CONV_TR = 256
CONV_HALO = 32
CONV_CHUNK = 32
MERGE_TM = 256
MLP_TM, MLP_TF = 512, 1024


def _params(semantics, vmem_mib):
    return pltpu.CompilerParams(dimension_semantics=semantics,
                                vmem_limit_bytes=vmem_mib * MIB)


def _layer_spec(shape, layer, index_map):
    return pl.BlockSpec((None,) + shape, lambda *g: (layer,) + index_map(*g))


def _rmsnorm_rows(x_ref, g_ref, h_ref):
    g = g_ref[...]

    def body(c, carry):
        r = pl.multiple_of(c * NORM_CHUNK, NORM_CHUNK)
        x = x_ref[pl.ds(r, NORM_CHUNK), :]
        ms = jnp.mean(x * x, axis=-1, keepdims=True)
        h_ref[pl.ds(r, NORM_CHUNK), :] = (x * lax.rsqrt(ms + RMS_EPS) * g).astype(h_ref.dtype)
        return carry

    lax.fori_loop(0, x_ref.shape[0] // NORM_CHUNK, body, 0, unroll=NORM_UNROLL)


def _in_proj_kernel(x_ref, g_ref, w_ref, o_ref, h_ref):
    @pl.when(pl.program_id(1) == 0)
    def _():
        _rmsnorm_rows(x_ref, g_ref, h_ref)

    o_ref[...] = jnp.dot(h_ref[...], w_ref[...], preferred_element_type=F32).astype(o_ref.dtype)


def _in_proj_out_block(j):
    sc, cf, gate = SRC_SC // IN_TN, SRC_CF // IN_TN, SRC_GATE // IN_TN
    return jnp.where(j >= gate, j - gate + COL_GATE // IN_TN,
                     jnp.where(j < sc, j + COL_Q // IN_TN,
                               jnp.where(j < cf, j - sc + COL_SC // IN_TN, j - cf + COL_CF // IN_TN)))


def _in_proj(x, g, w, layer):
    t = x.shape[0]
    return pl.pallas_call(
        _in_proj_kernel,
        grid=(t // IN_TM, N_IN // IN_TN),
        in_specs=[
            pl.BlockSpec((IN_TM, D_MODEL), lambda i, j: (i, 0)),
            _layer_spec((1, D_MODEL), layer, lambda i, j: (0, 0)),
            _layer_spec((D_MODEL, IN_TN), layer, lambda i, j: (0, j)),
        ],
        out_specs=pl.BlockSpec((IN_TM, IN_TN), lambda i, j: (i, _in_proj_out_block(j))),
        out_shape=jax.ShapeDtypeStruct((t, N_IN), BF16),
        scratch_shapes=[pltpu.VMEM((IN_TM, D_MODEL), BF16)],
        compiler_params=_params(("parallel", "arbitrary"), 56),
        name="in_proj",
    )(x, g, w)


def _neg_abs(x):
    bits = lax.bitcast_convert_type(x, jnp.uint32) | jnp.uint32(0x80000000)
    return lax.bitcast_convert_type(bits, F32)


def _sb_scan(z2, uo):
    f = jnp.maximum(z2, 0.0) + jnp.log(1.0 + jnp.exp2(_neg_abs(z2))) * LOG2_E
    hi = f.astype(BF16)
    lo = (f - hi.astype(F32)).astype(BF16)
    return [jnp.dot(jnp.concatenate([hi[:, c * LANES:(c + 1) * LANES], lo[:, c * LANES:(c + 1) * LANES]], axis=1),
                    uo, preferred_element_type=F32) for c in range(ATT_TQ // LANES)]


def _sb_weigh(z2, st, v, carry):
    n_sub = ATT_TQ // LANES
    a_parts = [None] * n_sub
    for c in reversed(range(n_sub)):
        suffix, total = st[c][:, :LANES], st[c][:, LANES:]
        if carry is not None:
            suffix = suffix + carry
        a_parts[c] = jnp.exp2(z2[:, c * LANES:(c + 1) * LANES] - suffix).astype(BF16)
        carry = total if carry is None else carry + total
    return carry, jnp.dot(jnp.concatenate(a_parts, axis=1), v, preferred_element_type=F32)


def _attn_kernel(q_ref, k_ref, v_ref, uo_ref, o_ref, carry_ref, acc_ref, pend_z_ref, pend_st_ref):
    qi = pl.program_id(2)
    uo = uo_ref[...]
    row = lax.broadcasted_iota(jnp.int32, (ATT_TQ, ATT_TQ), 0)
    col = lax.broadcasted_iota(jnp.int32, (ATT_TQ, ATT_TQ), 1)
    n = ATT_HEADS_PER_STEP
    n_sub = ATT_TQ // LANES
    heads = [slice(h * HEAD_DIM, (h + 1) * HEAD_DIM) for h in range(n)]
    qs = [q_ref[:, hs] for hs in heads]

    def rows(kb):
        return pl.ds(pl.multiple_of(kb * ATT_TQ, ATT_TQ), ATT_TQ)

    def weigh(h, z2, st, kb, start):
        carry, pv = _sb_weigh(z2, st, v_ref[rows(kb), heads[h]], None if start else carry_ref[h])
        carry_ref[h] = carry
        acc_ref[h] = pv if start else acc_ref[h] + pv

    def pair(kb, prev_kb, mask):
        z2s, sts = [None] * n, [None] * n
        for w in range(n):
            z2 = lax.dot_general(qs[w], k_ref[rows(kb), heads[w]], (((1,), (1,)), ((), ())),
                                 preferred_element_type=F32) * ATT_LOG2_SCALE
            z2s[w] = z2 if mask is None else jnp.where(mask, z2, MASKED_LOG2)
            if w >= 1:
                sts[w - 1] = _sb_scan(z2s[w - 1], uo)
            elif prev_kb is not None:
                pend_st_next = _sb_scan(pend_z_ref[1], uo)
            if w >= 2:
                weigh(w - 2, z2s[w - 2], sts[w - 2], kb, prev_kb is None)
            elif prev_kb is not None:
                st = [pend_st_ref[c] for c in range(n_sub)] if w == 0 else pend_st_next
                weigh(n - 2 + w, pend_z_ref[w], st, prev_kb, False)
        pend_z_ref[0] = z2s[n - 2]
        pend_z_ref[1] = z2s[n - 1]
        for c in range(n_sub):
            pend_st_ref[c] = sts[n - 2][c]

    for h in (n - 2, n - 1):
        carry_ref[h] = jnp.zeros((ATT_TQ, HEAD_DIM), F32)
        acc_ref[h] = jnp.zeros((ATT_TQ, HEAD_DIM), F32)
    pair(qi, None, col < row)

    def body(i, carry):
        kb = qi - 1 - i
        pair(kb, kb + 1, None)
        return carry

    lax.fori_loop(0, qi, body, 0)
    pend_st = _sb_scan(pend_z_ref[1], uo)
    weigh(n - 2, pend_z_ref[0], [pend_st_ref[c] for c in range(n_sub)], 0, False)
    weigh(n - 1, pend_z_ref[1], pend_st, 0, False)
    for h, hs in enumerate(heads):
        o_ref[:, hs] = acc_ref[h].astype(o_ref.dtype)


def _attention(proj, uo, batch, seq):
    nq = seq // ATT_TQ
    width = ATT_HEADS_PER_STEP * HEAD_DIM
    return pl.pallas_call(
        _attn_kernel,
        grid=(batch, N_HEADS // ATT_HEADS_PER_STEP, nq),
        in_specs=[
            pl.BlockSpec((ATT_TQ, width), lambda b, h, i: (b * nq + i, COL_Q // width + h)),
            pl.BlockSpec((seq, width), lambda b, h, i: (b, COL_K // width + h)),
            pl.BlockSpec((seq, width), lambda b, h, i: (b, COL_V // width + h)),
            pl.BlockSpec((2 * LANES, 2 * LANES), lambda b, h, i: (0, 0)),
        ],
        out_specs=pl.BlockSpec((ATT_TQ, width), lambda b, h, i: (b * nq + i, h)),
        out_shape=jax.ShapeDtypeStruct((batch * seq, D_ATT), BF16),
        scratch_shapes=[
            pltpu.VMEM((ATT_HEADS_PER_STEP, ATT_TQ, HEAD_DIM), F32),
            pltpu.VMEM((ATT_HEADS_PER_STEP, ATT_TQ, HEAD_DIM), F32),
            pltpu.VMEM((2, ATT_TQ, ATT_TQ), F32),
            pltpu.VMEM((ATT_TQ // LANES, ATT_TQ, 2 * LANES), F32),
        ],
        compiler_params=_params(("parallel", "parallel", "arbitrary"), 32),
        name="sb_attn",
    )(proj, proj, proj, uo)


def _conv_kernel(scb_ref, scc_ref, scu_ref, scc_h_ref, scu_h_ref, cf_ref, cf_h_ref,
                 wa_ref, wc_ref, bc_ref, lng_ref, lnb_ref, a_out_ref, c_out_ref, p_scr, u_scr):
    first = pl.program_id(1) == 0

    p_halo = scc_h_ref[...].astype(F32) * scu_h_ref[...].astype(F32)
    p_scr[:CONV_HALO, :] = jnp.where(first, 0.0, p_halo)
    p_scr[CONV_HALO:, :] = scc_ref[...].astype(F32) * scu_ref[...].astype(F32)

    def glu(ref):
        v = ref[...].astype(F32)
        return v[:, :D_CF] * jax.nn.sigmoid(v[:, D_CF:])

    u_scr[:CONV_HALO, :] = jnp.where(first, 0.0, glu(cf_h_ref))
    u_scr[CONV_HALO:, :] = glu(cf_ref)

    wa = wa_ref[...]
    wc = wc_ref[...]
    for c in range(CONV_TR // CONV_CHUNK):
        r0 = c * CONV_CHUNK
        rows = slice(r0, r0 + CONV_CHUNK)
        base = CONV_HALO - (SC_WIDTH - 1) + r0
        acc = wa[0:1, :] * p_scr[base:base + CONV_CHUNK, :]
        for k in range(1, SC_WIDTH):
            acc = acc + wa[k:k + 1, :] * p_scr[base + k:base + k + CONV_CHUNK, :]
        a_out_ref[rows, :] = (scb_ref[rows, :].astype(F32) * acc).astype(a_out_ref.dtype)
        base = CONV_HALO - (CF_WIDTH - 1) + r0
        acc = wc[0:1, :] * u_scr[base:base + CONV_CHUNK, :]
        for k in range(1, CF_WIDTH):
            acc = acc + wc[k:k + 1, :] * u_scr[base + k:base + k + CONV_CHUNK, :]
        acc = acc + bc_ref[...]
        mu = jnp.mean(acc, axis=-1, keepdims=True)
        cen = acc - mu
        var = jnp.mean(cen * cen, axis=-1, keepdims=True)
        y = cen * lax.rsqrt(var + LN_EPS) * lng_ref[...] + lnb_ref[...]
        c_out_ref[rows, :] = (y * jax.nn.sigmoid(y)).astype(c_out_ref.dtype)


def _conv_branches(proj, wa, wc, bc, lng, lnb, layer, batch, seq):
    nr = seq // CONV_TR
    hr = CONV_TR // CONV_HALO

    def cur(width, col):
        return pl.BlockSpec((CONV_TR, width), lambda b, i: (b * nr + i, col // width))

    def halo(width, col):
        return pl.BlockSpec((CONV_HALO, width),
                            lambda b, i: (jnp.maximum((b * nr + i) * hr - 1, 0), col // width))

    def whole(shape):
        return _layer_spec(shape, layer, lambda b, i: (0, 0))

    return pl.pallas_call(
        _conv_kernel,
        grid=(batch, nr),
        in_specs=[
            cur(D_SC, COL_SC), cur(D_SC, COL_SC + D_SC), cur(D_SC, COL_SC + 2 * D_SC),
            halo(D_SC, COL_SC + D_SC), halo(D_SC, COL_SC + 2 * D_SC),
            cur(2 * D_CF, COL_CF), halo(2 * D_CF, COL_CF),
            whole((SC_WIDTH, D_SC)), whole((CF_WIDTH, D_CF)),
            whole((1, D_CF)), whole((1, D_CF)), whole((1, D_CF)),
        ],
        out_specs=[
            pl.BlockSpec((CONV_TR, D_SC), lambda b, i: (b * nr + i, 0)),
            pl.BlockSpec((CONV_TR, D_CF), lambda b, i: (b * nr + i, 0)),
        ],
        out_shape=[
            jax.ShapeDtypeStruct((batch * seq, D_SC), BF16),
            jax.ShapeDtypeStruct((batch * seq, D_CF), BF16),
        ],
        scratch_shapes=[
            pltpu.VMEM((CONV_HALO + CONV_TR, D_SC), F32),
            pltpu.VMEM((CONV_HALO + CONV_TR, D_CF), F32),
        ],
        compiler_params=_params(("parallel", "arbitrary"), 32),
        name="conv_branches",
    )(proj, proj, proj, proj, proj, proj, proj, wa, wc, bc, lng, lnb)


def _merge_kernel(a_ref, att_ref, c_ref, ga_ref, gb_ref, gc_ref, x_ref,
                  pa_ref, pb_ref, pc_ref, wo_ref, g_ref, o_ref):
    def gated(gate_ref, feat_ref, w_ref):
        y = jnp.dot(feat_ref[...], w_ref[...], preferred_element_type=F32)
        return jax.nn.sigmoid(gate_ref[...].astype(F32)) * y

    mixed = gated(ga_ref, a_ref, pa_ref) + gated(gb_ref, att_ref, pb_ref) + gated(gc_ref, c_ref, pc_ref)
    m = jnp.dot(mixed.astype(BF16), wo_ref[...], preferred_element_type=F32)
    ms = jnp.mean(m * m, axis=-1, keepdims=True)
    o_ref[...] = x_ref[...] + m * lax.rsqrt(ms + RMS_EPS) * g_ref[...]


def _merge(a_in, att, c_in, proj, x, pa, pb, pc, wo, g, layer):
    t = x.shape[0]

    def rows(width, col=0):
        return pl.BlockSpec((MERGE_TM, width), lambda i: (i, col // width))

    def whole(shape):
        return _layer_spec(shape, layer, lambda i: (0, 0))

    return pl.pallas_call(
        _merge_kernel,
        grid=(t // MERGE_TM,),
        in_specs=[
            rows(D_SC), rows(D_ATT), rows(D_CF),
            rows(D_MODEL, COL_GATE), rows(D_MODEL, COL_GATE + D_MODEL), rows(D_MODEL, COL_GATE + 2 * D_MODEL),
            rows(D_MODEL),
            whole((D_SC, D_MODEL)), whole((D_ATT, D_MODEL)), whole((D_CF, D_MODEL)),
            whole((D_MODEL, D_MODEL)), whole((1, D_MODEL)),
        ],
        out_specs=rows(D_MODEL),
        out_shape=jax.ShapeDtypeStruct((t, D_MODEL), F32),
        compiler_params=_params(("parallel",), 56),
        name="merge",
    )(a_in, att, c_in, proj, proj, proj, x, pa, pb, pc, wo, g)


def _mlp_kernel(x_ref, gpre_ref, wu_ref, wd_ref, gpost_ref, o_ref, h_ref, acc_ref):
    f = pl.program_id(1)

    @pl.when(f == 0)
    def _():
        _rmsnorm_rows(x_ref, gpre_ref, h_ref)

    u = jnp.dot(h_ref[...], wu_ref[...], preferred_element_type=F32)
    r = jnp.square(jnp.maximum(u, 0.0)).astype(BF16)
    part = jnp.dot(r, wd_ref[...], preferred_element_type=F32)

    @pl.when(f == 0)
    def _():
        acc_ref[...] = part

    @pl.when(f != 0)
    def _():
        acc_ref[...] += part

    @pl.when(f == pl.num_programs(1) - 1)
    def _():
        y = acc_ref[...]
        ms = jnp.mean(y * y, axis=-1, keepdims=True)
        o_ref[...] = x_ref[...] + y * lax.rsqrt(ms + RMS_EPS) * gpost_ref[...]


def _mlp(x, gpre, wu, wd, gpost, layer):
    t = x.shape[0]
    return pl.pallas_call(
        _mlp_kernel,
        grid=(t // MLP_TM, D_FF // MLP_TF),
        in_specs=[
            pl.BlockSpec((MLP_TM, D_MODEL), lambda i, f: (i, 0)),
            _layer_spec((1, D_MODEL), layer, lambda i, f: (0, 0)),
            _layer_spec((D_MODEL, MLP_TF), layer, lambda i, f: (0, f)),
            _layer_spec((MLP_TF, D_MODEL), layer, lambda i, f: (f, 0)),
            _layer_spec((1, D_MODEL), layer, lambda i, f: (0, 0)),
        ],
        out_specs=pl.BlockSpec((MLP_TM, D_MODEL), lambda i, f: (i, 0)),
        out_shape=jax.ShapeDtypeStruct((t, D_MODEL), F32),
        scratch_shapes=[pltpu.VMEM((MLP_TM, D_MODEL), BF16), pltpu.VMEM((MLP_TM, D_MODEL), F32)],
        compiler_params=_params(("parallel", "arbitrary"), 48),
        name="mlp",
    )(x, gpre, wu, wd, gpost)


def _scan_matrix():
    j = jnp.arange(LANES)[:, None]
    s = jnp.arange(LANES)[None, :]
    half = jnp.concatenate([(j >= s).astype(BF16), jnp.ones((LANES, LANES), BF16)], axis=1)
    return jnp.concatenate([half, half], axis=0)


def kernel(x, ln_mix_pre, ln_mix_post, ln_mlp_pre, ln_mlp_post, w_in, conv_a_w, proj_a, proj_b,
           conv_c_w, conv_c_b, norm_c_g, norm_c_b, proj_c, w_o, w_up, w_down):
    batch, seq, d = x.shape
    depth = w_in.shape[0]
    assert d == D_MODEL and w_in.shape[2] == N_IN
    assert seq % ATT_TQ == 0 and seq % CONV_TR == 0 and (batch * seq) % IN_TM == 0
    uo = _scan_matrix()
    w_in, proj_a, proj_b, proj_c, w_o, w_up, w_down = (
        w.astype(BF16) for w in (w_in, proj_a, proj_b, proj_c, w_o, w_up, w_down))
    ln_mix_pre, ln_mix_post, ln_mlp_pre, ln_mlp_post, conv_c_b, norm_c_g, norm_c_b = (
        p[:, None, :] for p in (ln_mix_pre, ln_mix_post, ln_mlp_pre, ln_mlp_post, conv_c_b, norm_c_g, norm_c_b))
    xt = x.reshape(batch * seq, d)
    for l in range(depth):
        proj = _in_proj(xt, ln_mix_pre, w_in, l)
        att = _attention(proj, uo, batch, seq)
        a_in, c_in = _conv_branches(proj, conv_a_w, conv_c_w, conv_c_b, norm_c_g, norm_c_b, l, batch, seq)
        xt = _merge(a_in, att, c_in, proj, xt, proj_a, proj_b, proj_c, w_o, ln_mix_post, l)
        xt = _mlp(xt, ln_mlp_pre, w_up, w_down, ln_mlp_post, l)
    return xt.reshape(batch, seq, d)
```

```python
import math

import jax
import jax.numpy as jnp
from jax import lax
from jax.experimental import pallas as pl
from jax.experimental.pallas import tpu as pltpu

F32 = jnp.float32
BF16 = jnp.bfloat16

D_MODEL = 2048
HEAD_DIM = 128
D_ATT = D_MODEL // 2
N_HEADS = D_ATT // HEAD_DIM
D_SC = D_MODEL // 4
SC_WIDTH = 3
D_CF = D_MODEL // 4
CF_WIDTH = 31
D_FF = 4 * D_MODEL
N_IN = 3 * D_ATT + 3 * D_SC + 2 * D_CF + 3 * D_MODEL
RMS_EPS = 1e-6
LN_EPS = 1e-5

SRC_SC = 3 * D_ATT
SRC_CF = SRC_SC + 3 * D_SC
SRC_GATE = SRC_CF + 2 * D_CF
COL_GATE = 0
COL_Q = 3 * D_MODEL
COL_K = COL_Q + D_ATT
COL_V = COL_K + D_ATT
COL_CF = COL_V + D_ATT
COL_SC = COL_CF + 2 * D_CF
assert COL_SC + 3 * D_SC == N_IN

LANES = 128
SUBLANES = 8
MIB = 1024 * 1024

NORM_CHUNK = 16
NORM_UNROLL = 4
IN_TM, IN_TN = 2048, 512
ATT_TQ = 256
ATT_HEADS_PER_STEP = 8
ATT_LOG2_SCALE = HEAD_DIM ** -0.5 * math.log2(math.e)
LOG2_E = math.log2(math.e)
MASKED_LOG2 = -1e30
CONV_TR = 256
CONV_HALO = 32
CONV_CHUNK = 32
CONV_SHIFT_ROWS = CONV_HALO + CONV_TR - SUBLANES
MERGE_TM = 256
MLP_TM, MLP_TF = 512, 1024


def _params(semantics, vmem_mib):
    return pltpu.CompilerParams(dimension_semantics=semantics,
                                vmem_limit_bytes=vmem_mib * MIB)


def _layer_spec(shape, layer, index_map):
    return pl.BlockSpec((None,) + shape, lambda *g: (layer,) + index_map(*g))


def _rmsnorm_rows(x_ref, g_ref, h_ref):
    g = g_ref[...]

    def body(c, carry):
        r = pl.multiple_of(c * NORM_CHUNK, NORM_CHUNK)
        x = x_ref[pl.ds(r, NORM_CHUNK), :]
        ms = jnp.mean(x * x, axis=-1, keepdims=True)
        h_ref[pl.ds(r, NORM_CHUNK), :] = (x * lax.rsqrt(ms + RMS_EPS) * g).astype(h_ref.dtype)
        return carry

    lax.fori_loop(0, x_ref.shape[0] // NORM_CHUNK, body, 0, unroll=NORM_UNROLL)


def _in_proj_kernel(x_ref, g_ref, w_ref, o_ref, h_ref):
    @pl.when(pl.program_id(1) == 0)
    def _():
        _rmsnorm_rows(x_ref, g_ref, h_ref)

    w = w_ref[...].astype(BF16)
    o_ref[...] = jnp.dot(h_ref[...], w, preferred_element_type=F32).astype(o_ref.dtype)


def _in_proj_out_block(j):
    sc, cf, gate = SRC_SC // IN_TN, SRC_CF // IN_TN, SRC_GATE // IN_TN
    return jnp.where(j >= gate, j - gate + COL_GATE // IN_TN,
                     jnp.where(j < sc, j + COL_Q // IN_TN,
                               jnp.where(j < cf, j - sc + COL_SC // IN_TN, j - cf + COL_CF // IN_TN)))


def _in_proj(x, g, w, layer):
    t = x.shape[0]
    return pl.pallas_call(
        _in_proj_kernel,
        grid=(t // IN_TM, N_IN // IN_TN),
        in_specs=[
            pl.BlockSpec((IN_TM, D_MODEL), lambda i, j: (i, 0)),
            _layer_spec((1, D_MODEL), layer, lambda i, j: (0, 0)),
            _layer_spec((D_MODEL, IN_TN), layer, lambda i, j: (0, j)),
        ],
        out_specs=pl.BlockSpec((IN_TM, IN_TN), lambda i, j: (i, _in_proj_out_block(j))),
        out_shape=jax.ShapeDtypeStruct((t, N_IN), BF16),
        scratch_shapes=[pltpu.VMEM((IN_TM, D_MODEL), BF16)],
        compiler_params=_params(("parallel", "arbitrary"), 60),
        name="in_proj",
    )(x, g, w)


def _sb_scan(z2, uo):
    f = jnp.maximum(z2, 0.0) + jnp.log2(1.0 + jnp.exp2(-jnp.abs(z2)))
    hi = f.astype(BF16)
    lo = (f - hi.astype(F32)).astype(BF16)
    return [jnp.dot(jnp.concatenate([hi[:, c * LANES:(c + 1) * LANES], lo[:, c * LANES:(c + 1) * LANES]], axis=1),
                    uo, preferred_element_type=F32) for c in range(ATT_TQ // LANES)]


def _sb_weigh(z2, st, v, carry):
    n_sub = ATT_TQ // LANES
    a_parts = [None] * n_sub
    for c in reversed(range(n_sub)):
        suffix, total = st[c][:, :LANES], st[c][:, LANES:]
        if carry is not None:
            suffix = suffix + carry
        a_parts[c] = jnp.exp2(z2[:, c * LANES:(c + 1) * LANES] - suffix).astype(BF16)
        carry = total if carry is None else carry + total
    return carry, jnp.dot(jnp.concatenate(a_parts, axis=1), v, preferred_element_type=F32)


def _attn_kernel(q_ref, k_ref, v_ref, uo_ref, o_ref, carry_ref, acc_ref, pend_z_ref, pend_st_ref):
    qi = pl.program_id(2)
    uo = uo_ref[...]
    row = lax.broadcasted_iota(jnp.int32, (ATT_TQ, ATT_TQ), 0)
    col = lax.broadcasted_iota(jnp.int32, (ATT_TQ, ATT_TQ), 1)
    n = ATT_HEADS_PER_STEP
    n_sub = ATT_TQ // LANES
    heads = [slice(h * HEAD_DIM, (h + 1) * HEAD_DIM) for h in range(n)]
    qs = [q_ref[:, hs] for hs in heads]

    def rows(kb):
        return pl.ds(pl.multiple_of(kb * ATT_TQ, ATT_TQ), ATT_TQ)

    def weigh(h, z2, st, kb, start):
        carry, pv = _sb_weigh(z2, st, v_ref[rows(kb), heads[h]], None if start else carry_ref[h])
        carry_ref[h] = carry
        acc_ref[h] = pv if start else acc_ref[h] + pv

    def pair(kb, prev_kb, mask):
        z2s, sts = [None] * n, [None] * n
        for w in range(n):
            z2 = lax.dot_general(qs[w], k_ref[rows(kb), heads[w]], (((1,), (1,)), ((), ())),
                                 preferred_element_type=F32) * ATT_LOG2_SCALE
            z2s[w] = z2 if mask is None else jnp.where(mask, z2, MASKED_LOG2)
            if w >= 1:
                sts[w - 1] = _sb_scan(z2s[w - 1], uo)
            elif prev_kb is not None:
                pend_st_next = _sb_scan(pend_z_ref[1], uo)
            if w >= 2:
                weigh(w - 2, z2s[w - 2], sts[w - 2], kb, prev_kb is None)
            elif prev_kb is not None:
                st = [pend_st_ref[c] for c in range(n_sub)] if w == 0 else pend_st_next
                weigh(n - 2 + w, pend_z_ref[w], st, prev_kb, False)
        pend_z_ref[0] = z2s[n - 2]
        pend_z_ref[1] = z2s[n - 1]
        for c in range(n_sub):
            pend_st_ref[c] = sts[n - 2][c]

    for h in (n - 2, n - 1):
        carry_ref[h] = jnp.zeros((ATT_TQ, HEAD_DIM), F32)
        acc_ref[h] = jnp.zeros((ATT_TQ, HEAD_DIM), F32)
    pair(qi, None, col < row)

    def body(i, carry):
        kb = qi - 1 - i
        pair(kb, kb + 1, None)
        return carry

    lax.fori_loop(0, qi, body, 0)
    pend_st = _sb_scan(pend_z_ref[1], uo)
    weigh(n - 2, pend_z_ref[0], [pend_st_ref[c] for c in range(n_sub)], 0, False)
    weigh(n - 1, pend_z_ref[1], pend_st, 0, False)
    for h, hs in enumerate(heads):
        o_ref[:, hs] = acc_ref[h].astype(o_ref.dtype)


def _attention(proj, uo, batch, seq):
    nq = seq // ATT_TQ
    width = ATT_HEADS_PER_STEP * HEAD_DIM
    return pl.pallas_call(
        _attn_kernel,
        grid=(batch, N_HEADS // ATT_HEADS_PER_STEP, nq),
        in_specs=[
            pl.BlockSpec((ATT_TQ, width), lambda b, h, i: (b * nq + i, COL_Q // width + h)),
            pl.BlockSpec((seq, width), lambda b, h, i: (b, COL_K // width + h)),
            pl.BlockSpec((seq, width), lambda b, h, i: (b, COL_V // width + h)),
            pl.BlockSpec((2 * LANES, 2 * LANES), lambda b, h, i: (0, 0)),
        ],
        out_specs=pl.BlockSpec((ATT_TQ, width), lambda b, h, i: (b * nq + i, h)),
        out_shape=jax.ShapeDtypeStruct((batch * seq, D_ATT), BF16),
        scratch_shapes=[
            pltpu.VMEM((ATT_HEADS_PER_STEP, ATT_TQ, HEAD_DIM), F32),
            pltpu.VMEM((ATT_HEADS_PER_STEP, ATT_TQ, HEAD_DIM), F32),
            pltpu.VMEM((2, ATT_TQ, ATT_TQ), F32),
            pltpu.VMEM((ATT_TQ // LANES, ATT_TQ, 2 * LANES), F32),
        ],
        compiler_params=_params(("parallel", "parallel", "arbitrary"), 32),
        name="sb_attn",
    )(proj, proj, proj, uo)


def _conv_kernel(scb_ref, scc_ref, scu_ref, scc_h_ref, scu_h_ref, cf_ref, cf_h_ref,
                 wa_ref, wc_ref, bc_ref, lng_ref, lnb_ref, a_out_ref, c_out_ref, p_scr, u_scr):
    first = pl.program_id(1) == 0

    p_halo = scc_h_ref[...].astype(F32) * scu_h_ref[...].astype(F32)
    p_scr[:CONV_HALO, :] = jnp.where(first, 0.0, p_halo)
    p_scr[CONV_HALO:, :] = scc_ref[...].astype(F32) * scu_ref[...].astype(F32)

    def glu(ref):
        v = ref[...].astype(F32)
        return v[:, :D_CF] * jax.nn.sigmoid(v[:, D_CF:])

    u_scr[0, :CONV_HALO, :] = jnp.where(first, 0.0, glu(cf_h_ref))
    u_scr[0, CONV_HALO:, :] = glu(cf_ref)
    for b in range(1, SUBLANES):
        u_scr[b, :CONV_SHIFT_ROWS, :] = u_scr[0, b:b + CONV_SHIFT_ROWS, :]

    wa = wa_ref[...]
    wc = wc_ref[...]
    for c in range(CONV_TR // CONV_CHUNK):
        r0 = c * CONV_CHUNK
        rows = slice(r0, r0 + CONV_CHUNK)
        base = CONV_HALO - (SC_WIDTH - 1) + r0
        acc = wa[0:1, :] * p_scr[base:base + CONV_CHUNK, :]
        for k in range(1, SC_WIDTH):
            acc = acc + wa[k:k + 1, :] * p_scr[base + k:base + k + CONV_CHUNK, :]
        a_out_ref[rows, :] = (scb_ref[rows, :].astype(F32) * acc).astype(a_out_ref.dtype)
        base = CONV_HALO - (CF_WIDTH - 1) + r0
        acc = None
        for k in range(CF_WIDTH):
            shift, start = (base + k) % SUBLANES, (base + k) // SUBLANES * SUBLANES
            tap = wc[k:k + 1, :] * u_scr[shift, start:start + CONV_CHUNK, :]
            acc = tap if acc is None else acc + tap
        acc = acc + bc_ref[...]
        mu = jnp.mean(acc, axis=-1, keepdims=True)
        cen = acc - mu
        var = jnp.mean(cen * cen, axis=-1, keepdims=True)
        y = cen * lax.rsqrt(var + LN_EPS) * lng_ref[...] + lnb_ref[...]
        c_out_ref[rows, :] = (y * jax.nn.sigmoid(y)).astype(c_out_ref.dtype)


def _conv_branches(proj, wa, wc, bc, lng, lnb, layer, batch, seq):
    nr = seq // CONV_TR
    hr = CONV_TR // CONV_HALO

    def cur(width, col):
        return pl.BlockSpec((CONV_TR, width), lambda b, i: (b * nr + i, col // width))

    def halo(width, col):
        return pl.BlockSpec((CONV_HALO, width),
                            lambda b, i: (jnp.maximum((b * nr + i) * hr - 1, 0), col // width))

    def whole(shape):
        return _layer_spec(shape, layer, lambda b, i: (0, 0))

    return pl.pallas_call(
        _conv_kernel,
        grid=(batch, nr),
        in_specs=[
            cur(D_SC, COL_SC), cur(D_SC, COL_SC + D_SC), cur(D_SC, COL_SC + 2 * D_SC),
            halo(D_SC, COL_SC + D_SC), halo(D_SC, COL_SC + 2 * D_SC),
            cur(2 * D_CF, COL_CF), halo(2 * D_CF, COL_CF),
            whole((SC_WIDTH, D_SC)), whole((CF_WIDTH, D_CF)),
            whole((1, D_CF)), whole((1, D_CF)), whole((1, D_CF)),
        ],
        out_specs=[
            pl.BlockSpec((CONV_TR, D_SC), lambda b, i: (b * nr + i, 0)),
            pl.BlockSpec((CONV_TR, D_CF), lambda b, i: (b * nr + i, 0)),
        ],
        out_shape=[
            jax.ShapeDtypeStruct((batch * seq, D_SC), BF16),
            jax.ShapeDtypeStruct((batch * seq, D_CF), BF16),
        ],
        scratch_shapes=[
            pltpu.VMEM((CONV_HALO + CONV_TR, D_SC), F32),
            pltpu.VMEM((SUBLANES, CONV_HALO + CONV_TR, D_CF), F32),
        ],
        compiler_params=_params(("parallel", "arbitrary"), 32),
        name="conv_branches",
    )(proj, proj, proj, proj, proj, proj, proj, wa, wc, bc, lng, lnb)


def _merge_kernel(a_ref, att_ref, c_ref, ga_ref, gb_ref, gc_ref, x_ref,
                  pa_ref, pb_ref, pc_ref, wo_ref, g_ref, o_ref):
    def gated(gate_ref, feat_ref, w_ref):
        y = jnp.dot(feat_ref[...], w_ref[...], preferred_element_type=F32)
        return jax.nn.sigmoid(gate_ref[...].astype(F32)) * y

    mixed = gated(ga_ref, a_ref, pa_ref) + gated(gb_ref, att_ref, pb_ref) + gated(gc_ref, c_ref, pc_ref)
    m = jnp.dot(mixed.astype(BF16), wo_ref[...], preferred_element_type=F32)
    ms = jnp.mean(m * m, axis=-1, keepdims=True)
    o_ref[...] = x_ref[...] + m * lax.rsqrt(ms + RMS_EPS) * g_ref[...]


def _merge(a_in, att, c_in, proj, x, pa, pb, pc, wo, g, layer):
    t = x.shape[0]

    def rows(width, col=0):
        return pl.BlockSpec((MERGE_TM, width), lambda i: (i, col // width))

    def whole(shape):
        return _layer_spec(shape, layer, lambda i: (0, 0))

    return pl.pallas_call(
        _merge_kernel,
        grid=(t // MERGE_TM,),
        in_specs=[
            rows(D_SC), rows(D_ATT), rows(D_CF),
            rows(D_MODEL, COL_GATE), rows(D_MODEL, COL_GATE + D_MODEL), rows(D_MODEL, COL_GATE + 2 * D_MODEL),
            rows(D_MODEL),
            whole((D_SC, D_MODEL)), whole((D_ATT, D_MODEL)), whole((D_CF, D_MODEL)),
            whole((D_MODEL, D_MODEL)), whole((1, D_MODEL)),
        ],
        out_specs=rows(D_MODEL),
        out_shape=jax.ShapeDtypeStruct((t, D_MODEL), F32),
        compiler_params=_params(("parallel",), 56),
        name="merge",
    )(a_in, att, c_in, proj, proj, proj, x, pa, pb, pc, wo, g)


def _mlp_kernel(x_ref, gpre_ref, wu_ref, wd_ref, gpost_ref, o_ref, h_ref, acc_ref):
    f = pl.program_id(1)

    @pl.when(f == 0)
    def _():
        _rmsnorm_rows(x_ref, gpre_ref, h_ref)
        acc_ref[...] = jnp.zeros_like(acc_ref)

    u = jnp.dot(h_ref[...], wu_ref[...], preferred_element_type=F32)
    r = jnp.square(jnp.maximum(u, 0.0)).astype(BF16)
    acc_ref[...] += jnp.dot(r, wd_ref[...], preferred_element_type=F32)

    @pl.when(f == pl.num_programs(1) - 1)
    def _():
        y = acc_ref[...]
        ms = jnp.mean(y * y, axis=-1, keepdims=True)
        o_ref[...] = x_ref[...] + y * lax.rsqrt(ms + RMS_EPS) * gpost_ref[...]


def _mlp(x, gpre, wu, wd, gpost, layer):
    t = x.shape[0]
    return pl.pallas_call(
        _mlp_kernel,
        grid=(t // MLP_TM, D_FF // MLP_TF),
        in_specs=[
            pl.BlockSpec((MLP_TM, D_MODEL), lambda i, f: (i, 0)),
            _layer_spec((1, D_MODEL), layer, lambda i, f: (0, 0)),
            _layer_spec((D_MODEL, MLP_TF), layer, lambda i, f: (0, f)),
            _layer_spec((MLP_TF, D_MODEL), layer, lambda i, f: (f, 0)),
            _layer_spec((1, D_MODEL), layer, lambda i, f: (0, 0)),
        ],
        out_specs=pl.BlockSpec((MLP_TM, D_MODEL), lambda i, f: (i, 0)),
        out_shape=jax.ShapeDtypeStruct((t, D_MODEL), F32),
        scratch_shapes=[pltpu.VMEM((MLP_TM, D_MODEL), BF16), pltpu.VMEM((MLP_TM, D_MODEL), F32)],
        compiler_params=_params(("parallel", "arbitrary"), 48),
        name="mlp",
    )(x, gpre, wu, wd, gpost)


def _scan_matrix():
    j = jnp.arange(LANES)[:, None]
    s = jnp.arange(LANES)[None, :]
    half = jnp.concatenate([(j >= s).astype(BF16), jnp.ones((LANES, LANES), BF16)], axis=1)
    return jnp.concatenate([half, half], axis=0)


def kernel(x, ln_mix_pre, ln_mix_post, ln_mlp_pre, ln_mlp_post, w_in, conv_a_w, proj_a, proj_b,
           conv_c_w, conv_c_b, norm_c_g, norm_c_b, proj_c, w_o, w_up, w_down):
    batch, seq, d = x.shape
    depth = w_in.shape[0]
    assert d == D_MODEL and w_in.shape[2] == N_IN
    assert seq % ATT_TQ == 0 and seq % CONV_TR == 0 and (batch * seq) % IN_TM == 0
    uo = _scan_matrix()
    proj_a, proj_b, proj_c, w_o, w_up, w_down = (
        w.astype(BF16) for w in (proj_a, proj_b, proj_c, w_o, w_up, w_down))
    ln_mix_pre, ln_mix_post, ln_mlp_pre, ln_mlp_post, conv_c_b, norm_c_g, norm_c_b = (
        p[:, None, :] for p in (ln_mix_pre, ln_mix_post, ln_mlp_pre, ln_mlp_post, conv_c_b, norm_c_g, norm_c_b))
    xt = x.reshape(batch * seq, d)
    for l in range(depth):
        proj = _in_proj(xt, ln_mix_pre, w_in, l)
        att = _attention(proj, uo, batch, seq)
        a_in, c_in = _conv_branches(proj, conv_a_w, conv_c_w, conv_c_b, norm_c_g, norm_c_b, l, batch, seq)
        xt = _merge(a_in, att, c_in, proj, xt, proj_a, proj_b, proj_c, w_o, ln_mix_post, l)
        xt = _mlp(xt, ln_mlp_pre, w_up, w_down, ln_mlp_post, l)
    return xt.reshape(batch, seq, d)
```

```python
import functools
import math

import jax
import jax.numpy as jnp
from jax import lax
from jax.experimental import pallas as pl
from jax.experimental.pallas import tpu as pltpu

F32 = jnp.float32
BF16 = jnp.bfloat16

D_MODEL = 2048
HEAD_DIM = 128
D_ATT = D_MODEL // 2
N_HEADS = D_ATT // HEAD_DIM
D_SC = D_MODEL // 4
SC_WIDTH = 3
D_CF = D_MODEL // 4
CF_WIDTH = 31
D_FF = 4 * D_MODEL
N_IN = 3 * D_ATT + 3 * D_SC + 2 * D_CF + 3 * D_MODEL
RMS_EPS = 1e-6
LN_EPS = 1e-5

SRC_SC = 3 * D_ATT
SRC_CF = SRC_SC + 3 * D_SC
SRC_GATE = SRC_CF + 2 * D_CF
COL_GATE = 0
COL_Q = 3 * D_MODEL
COL_K = COL_Q + D_ATT
COL_V = COL_K + D_ATT
COL_CF = COL_V + D_ATT
COL_SC = COL_CF + 2 * D_CF
assert COL_SC + 3 * D_SC == N_IN

LANES = 128
SUBLANES = 8
BF16_SUBLANES = 16
MIB = 1024 * 1024

NORM_CHUNK = 16
NORM_UNROLL = 4
IN_TM, IN_TN = 2048, 512
ATT_TQ = 256
ATT_HEADS_PER_STEP = 8
ATT_LOG2_SCALE = HEAD_DIM ** -0.5 * math.log2(math.e)
LOG2_E = math.log2(math.e)
MASKED_LOG2 = -1e30
CONV_TR = 256
CONV_HALO = 32
CONV_CHUNK = 32
CONV_SHIFT_ROWS = CONV_HALO + CONV_TR - SUBLANES
MERGE_TM = 256
MLP_TM, MLP_TF = 512, 1024


def _params(semantics, vmem_mib):
    return pltpu.CompilerParams(dimension_semantics=semantics,
                                vmem_limit_bytes=vmem_mib * MIB)


def _layer_spec(shape, layer, index_map):
    return pl.BlockSpec((None,) + shape, lambda *g: (layer,) + index_map(*g))


def _rmsnorm_rows(x_ref, g_ref, h_ref):
    g = g_ref[...]

    def body(c, carry):
        r = pl.multiple_of(c * NORM_CHUNK, NORM_CHUNK)
        x = x_ref[pl.ds(r, NORM_CHUNK), :]
        ms = jnp.mean(x * x, axis=-1, keepdims=True)
        h_ref[pl.ds(r, NORM_CHUNK), :] = (x * lax.rsqrt(ms + RMS_EPS) * g).astype(h_ref.dtype)
        return carry

    lax.fori_loop(0, x_ref.shape[0] // NORM_CHUNK, body, 0, unroll=NORM_UNROLL)


def _in_proj_kernel(x_ref, g_ref, w_ref, o_ref, h_ref):
    @pl.when(pl.program_id(1) == 0)
    def _():
        _rmsnorm_rows(x_ref, g_ref, h_ref)

    w = w_ref[...].astype(BF16)
    o_ref[...] = jnp.dot(h_ref[...], w, preferred_element_type=F32).astype(o_ref.dtype)


def _in_proj_out_block(j):
    sc, cf, gate = SRC_SC // IN_TN, SRC_CF // IN_TN, SRC_GATE // IN_TN
    return jnp.where(j >= gate, j - gate + COL_GATE // IN_TN,
                     jnp.where(j < sc, j + COL_Q // IN_TN,
                               jnp.where(j < cf, j - sc + COL_SC // IN_TN, j - cf + COL_CF // IN_TN)))


def _in_proj(x, g, w, layer):
    t = x.shape[0]
    return pl.pallas_call(
        _in_proj_kernel,
        grid=(t // IN_TM, N_IN // IN_TN),
        in_specs=[
            pl.BlockSpec((IN_TM, D_MODEL), lambda i, j: (i, 0)),
            _layer_spec((1, D_MODEL), layer, lambda i, j: (0, 0)),
            _layer_spec((D_MODEL, IN_TN), layer, lambda i, j: (0, j)),
        ],
        out_specs=pl.BlockSpec((IN_TM, IN_TN), lambda i, j: (i, _in_proj_out_block(j))),
        out_shape=jax.ShapeDtypeStruct((t, N_IN), BF16),
        scratch_shapes=[pltpu.VMEM((IN_TM, D_MODEL), BF16)],
        compiler_params=_params(("parallel", "arbitrary"), 60),
        name="in_proj",
    )(x, g, w)


def _sb_scan(z2, uo):
    f = jnp.maximum(z2, 0.0) + jnp.log2(1.0 + jnp.exp2(-jnp.abs(z2)))
    hi = f.astype(BF16)
    lo = (f - hi.astype(F32)).astype(BF16)
    return [jnp.dot(jnp.concatenate([hi[:, c * LANES:(c + 1) * LANES], lo[:, c * LANES:(c + 1) * LANES]], axis=1),
                    uo, preferred_element_type=F32) for c in range(ATT_TQ // LANES)]


def _sb_weigh(z2, st, v, carry):
    n_sub = ATT_TQ // LANES
    a_parts = [None] * n_sub
    for c in reversed(range(n_sub)):
        suffix, total = st[c][:, :LANES], st[c][:, LANES:]
        if carry is not None:
            suffix = suffix + carry
        a_parts[c] = jnp.exp2(z2[:, c * LANES:(c + 1) * LANES] - suffix).astype(BF16)
        carry = total if carry is None else carry + total
    return carry, jnp.dot(jnp.concatenate(a_parts, axis=1), v, preferred_element_type=F32)


def _attn_kernel(q_ref, k_ref, v_ref, uo_ref, o_ref, carry_ref, acc_ref, pend_z_ref, pend_st_ref):
    qi = pl.program_id(2)
    uo = uo_ref[...]
    row = lax.broadcasted_iota(jnp.int32, (ATT_TQ, ATT_TQ), 0)
    col = lax.broadcasted_iota(jnp.int32, (ATT_TQ, ATT_TQ), 1)
    n = ATT_HEADS_PER_STEP
    n_sub = ATT_TQ // LANES
    heads = [slice(h * HEAD_DIM, (h + 1) * HEAD_DIM) for h in range(n)]
    qs = [q_ref[:, hs] for hs in heads]

    def rows(kb):
        return pl.ds(pl.multiple_of(kb * ATT_TQ, ATT_TQ), ATT_TQ)

    def weigh(h, z2, st, kb, start):
        carry, pv = _sb_weigh(z2, st, v_ref[rows(kb), heads[h]], None if start else carry_ref[h])
        carry_ref[h] = carry
        acc_ref[h] = pv if start else acc_ref[h] + pv

    def pair(kb, prev_kb, mask):
        z2s, sts = [None] * n, [None] * n
        for w in range(n):
            z2 = lax.dot_general(qs[w], k_ref[rows(kb), heads[w]], (((1,), (1,)), ((), ())),
                                 preferred_element_type=F32) * ATT_LOG2_SCALE
            z2s[w] = z2 if mask is None else jnp.where(mask, z2, MASKED_LOG2)
            if w >= 1:
                sts[w - 1] = _sb_scan(z2s[w - 1], uo)
            elif prev_kb is not None:
                pend_st_next = _sb_scan(pend_z_ref[1], uo)
            if w >= 2:
                weigh(w - 2, z2s[w - 2], sts[w - 2], kb, prev_kb is None)
            elif prev_kb is not None:
                st = [pend_st_ref[c] for c in range(n_sub)] if w == 0 else pend_st_next
                weigh(n - 2 + w, pend_z_ref[w], st, prev_kb, False)
        pend_z_ref[0] = z2s[n - 2]
        pend_z_ref[1] = z2s[n - 1]
        for c in range(n_sub):
            pend_st_ref[c] = sts[n - 2][c]

    for h in (n - 2, n - 1):
        carry_ref[h] = jnp.zeros((ATT_TQ, HEAD_DIM), F32)
        acc_ref[h] = jnp.zeros((ATT_TQ, HEAD_DIM), F32)
    pair(qi, None, col < row)

    def body(i, carry):
        kb = qi - 1 - i
        pair(kb, kb + 1, None)
        return carry

    lax.fori_loop(0, qi, body, 0)
    pend_st = _sb_scan(pend_z_ref[1], uo)
    weigh(n - 2, pend_z_ref[0], [pend_st_ref[c] for c in range(n_sub)], 0, False)
    weigh(n - 1, pend_z_ref[1], pend_st, 0, False)
    for h, hs in enumerate(heads):
        o_ref[:, hs] = acc_ref[h].astype(o_ref.dtype)


def _attention(proj, uo, batch, seq):
    nq = seq // ATT_TQ
    width = ATT_HEADS_PER_STEP * HEAD_DIM
    return pl.pallas_call(
        _attn_kernel,
        grid=(batch, N_HEADS // ATT_HEADS_PER_STEP, nq),
        in_specs=[
            pl.BlockSpec((ATT_TQ, width), lambda b, h, i: (b * nq + i, COL_Q // width + h)),
            pl.BlockSpec((seq, width), lambda b, h, i: (b, COL_K // width + h)),
            pl.BlockSpec((seq, width), lambda b, h, i: (b, COL_V // width + h)),
            pl.BlockSpec((2 * LANES, 2 * LANES), lambda b, h, i: (0, 0)),
        ],
        out_specs=pl.BlockSpec((ATT_TQ, width), lambda b, h, i: (b * nq + i, h)),
        out_shape=jax.ShapeDtypeStruct((batch * seq, D_ATT), BF16),
        scratch_shapes=[
            pltpu.VMEM((ATT_HEADS_PER_STEP, ATT_TQ, HEAD_DIM), F32),
            pltpu.VMEM((ATT_HEADS_PER_STEP, ATT_TQ, HEAD_DIM), F32),
            pltpu.VMEM((2, ATT_TQ, ATT_TQ), F32),
            pltpu.VMEM((ATT_TQ // LANES, ATT_TQ, 2 * LANES), F32),
        ],
        compiler_params=_params(("parallel", "parallel", "arbitrary"), 32),
        name="sb_attn",
    )(proj, proj, proj, uo)


def _conv_kernel(scb_ref, scc_ref, scu_ref, scc_h_ref, scu_h_ref, cf_ref, cf_h_ref,
                 wa_ref, wc_ref, bc_ref, lng_ref, lnb_ref, a_out_ref, c_out_ref, p_scr, u_scr):
    first = pl.program_id(1) == 0

    p_halo = scc_h_ref[...].astype(F32) * scu_h_ref[...].astype(F32)
    p_scr[:CONV_HALO, :] = jnp.where(first, 0.0, p_halo)
    p_scr[CONV_HALO:, :] = scc_ref[...].astype(F32) * scu_ref[...].astype(F32)

    def glu(ref):
        v = ref[...].astype(F32)
        return v[:, :D_CF] * jax.nn.sigmoid(v[:, D_CF:])

    u_scr[0, :CONV_HALO, :] = jnp.where(first, 0.0, glu(cf_h_ref))
    u_scr[0, CONV_HALO:, :] = glu(cf_ref)
    for b in range(1, SUBLANES):
        u_scr[b, :CONV_SHIFT_ROWS, :] = u_scr[0, b:b + CONV_SHIFT_ROWS, :]

    wa = wa_ref[...]
    wc = wc_ref[...]
    for c in range(CONV_TR // CONV_CHUNK):
        r0 = c * CONV_CHUNK
        rows = slice(r0, r0 + CONV_CHUNK)
        base = CONV_HALO - (SC_WIDTH - 1) + r0
        acc = wa[0:1, :] * p_scr[base:base + CONV_CHUNK, :]
        for k in range(1, SC_WIDTH):
            acc = acc + wa[k:k + 1, :] * p_scr[base + k:base + k + CONV_CHUNK, :]
        a_out_ref[rows, :] = (scb_ref[rows, :].astype(F32) * acc).astype(a_out_ref.dtype)
        base = CONV_HALO - (CF_WIDTH - 1) + r0
        acc = None
        for k in range(CF_WIDTH):
            shift, start = (base + k) % SUBLANES, (base + k) // SUBLANES * SUBLANES
            tap = wc[k:k + 1, :] * u_scr[shift, start:start + CONV_CHUNK, :]
            acc = tap if acc is None else acc + tap
        acc = acc + bc_ref[...]
        mu = jnp.mean(acc, axis=-1, keepdims=True)
        cen = acc - mu
        var = jnp.mean(cen * cen, axis=-1, keepdims=True)
        y = cen * lax.rsqrt(var + LN_EPS) * lng_ref[...] + lnb_ref[...]
        c_out_ref[rows, :] = (y * jax.nn.sigmoid(y)).astype(c_out_ref.dtype)


def _conv_branches(proj, wa, wc, bc, lng, lnb, layer, batch, seq):
    nr = seq // CONV_TR
    hr = CONV_TR // CONV_HALO

    def cur(width, col):
        return pl.BlockSpec((CONV_TR, width), lambda b, i: (b * nr + i, col // width))

    def halo(width, col):
        return pl.BlockSpec((CONV_HALO, width),
                            lambda b, i: (jnp.maximum((b * nr + i) * hr - 1, 0), col // width))

    def whole(shape):
        return _layer_spec(shape, layer, lambda b, i: (0, 0))

    return pl.pallas_call(
        _conv_kernel,
        grid=(batch, nr),
        in_specs=[
            cur(D_SC, COL_SC), cur(D_SC, COL_SC + D_SC), cur(D_SC, COL_SC + 2 * D_SC),
            halo(D_SC, COL_SC + D_SC), halo(D_SC, COL_SC + 2 * D_SC),
            cur(2 * D_CF, COL_CF), halo(2 * D_CF, COL_CF),
            whole((SC_WIDTH, D_SC)), whole((CF_WIDTH, D_CF)),
            whole((1, D_CF)), whole((1, D_CF)), whole((1, D_CF)),
        ],
        out_specs=[
            pl.BlockSpec((CONV_TR, D_SC), lambda b, i: (b * nr + i, 0)),
            pl.BlockSpec((CONV_TR, D_CF), lambda b, i: (b * nr + i, 0)),
        ],
        out_shape=[
            jax.ShapeDtypeStruct((batch * seq, D_SC), BF16),
            jax.ShapeDtypeStruct((batch * seq, D_CF), BF16),
        ],
        scratch_shapes=[
            pltpu.VMEM((CONV_HALO + CONV_TR, D_SC), F32),
            pltpu.VMEM((SUBLANES, CONV_HALO + CONV_TR, D_CF), F32),
        ],
        compiler_params=_params(("parallel", "arbitrary"), 32),
        name="conv_branches",
    )(proj, proj, proj, proj, proj, proj, proj, wa, wc, bc, lng, lnb)


def _merge_kernel(a_ref, att_ref, c_ref, ga_ref, gb_ref, gc_ref, x_ref,
                  pa_ref, pb_ref, pc_ref, wo_ref, g_ref, o_ref):
    def gated(gate_ref, feat_ref, w_ref):
        y = jnp.dot(feat_ref[...], w_ref[...], preferred_element_type=F32)
        return jax.nn.sigmoid(gate_ref[...].astype(F32)) * y

    mixed = gated(ga_ref, a_ref, pa_ref) + gated(gb_ref, att_ref, pb_ref) + gated(gc_ref, c_ref, pc_ref)
    m = jnp.dot(mixed.astype(BF16), wo_ref[...], preferred_element_type=F32)
    ms = jnp.mean(m * m, axis=-1, keepdims=True)
    o_ref[...] = x_ref[...] + m * lax.rsqrt(ms + RMS_EPS) * g_ref[...]


def _merge(a_in, att, c_in, proj, x, pa, pb, pc, wo, g, layer):
    t = x.shape[0]

    def rows(width, col=0):
        return pl.BlockSpec((MERGE_TM, width), lambda i: (i, col // width))

    def whole(shape):
        return pl.BlockSpec(shape, lambda i: (0, 0))

    return pl.pallas_call(
        _merge_kernel,
        grid=(t // MERGE_TM,),
        in_specs=[
            rows(D_SC), rows(D_ATT), rows(D_CF),
            rows(D_MODEL, COL_GATE), rows(D_MODEL, COL_GATE + D_MODEL), rows(D_MODEL, COL_GATE + 2 * D_MODEL),
            rows(D_MODEL),
            whole((D_SC, D_MODEL)), whole((D_ATT, D_MODEL)), whole((D_CF, D_MODEL)),
            whole((D_MODEL, D_MODEL)), _layer_spec((1, D_MODEL), layer, lambda i: (0, 0)),
        ],
        out_specs=rows(D_MODEL),
        out_shape=jax.ShapeDtypeStruct((t, D_MODEL), F32),
        compiler_params=_params(("parallel",), 56),
        name="merge",
    )(a_in, att, c_in, proj, proj, proj, x, pa, pb, pc, wo, g)


def _mlp_kernel(x_ref, gpre_ref, wu_ref, wd_ref, gpost_ref, *refs, n_cast):
    src_refs, o_ref, dst_refs, (h_ref, acc_ref) = refs[:n_cast], refs[n_cast], refs[n_cast + 1:-2], refs[-2:]
    f = pl.program_id(1)

    @pl.when(f == 0)
    def _():
        _rmsnorm_rows(x_ref, gpre_ref, h_ref)
        acc_ref[...] = jnp.zeros_like(acc_ref)

    u = jnp.dot(h_ref[...], wu_ref[...], preferred_element_type=F32)
    r = jnp.square(jnp.maximum(u, 0.0)).astype(BF16)
    acc_ref[...] += jnp.dot(r, wd_ref[...], preferred_element_type=F32)
    for src_ref, dst_ref in zip(src_refs, dst_refs):
        dst_ref[...] = src_ref[...].astype(dst_ref.dtype)

    @pl.when(f == pl.num_programs(1) - 1)
    def _():
        y = acc_ref[...]
        ms = jnp.mean(y * y, axis=-1, keepdims=True)
        o_ref[...] = x_ref[...] + y * lax.rsqrt(ms + RMS_EPS) * gpost_ref[...]


def _mlp(x, gpre, wu, wd, gpost, layer, next_weights):
    t = x.shape[0]
    grid = (t // MLP_TM, D_FF // MLP_TF)
    n_steps = grid[0] * grid[1]
    cast_in, cast_out, cast_shapes = [], [], []
    for w in next_weights:
        rows, cols = w.shape[1:]
        rb = max(rows // n_steps, BF16_SUBLANES)
        assert rows % rb == 0 and rows // rb <= n_steps
        last = rows // rb - 1

        def block(i, f, last=last):
            return (jnp.minimum(i * grid[1] + f, last), 0)

        cast_in.append(_layer_spec((rb, cols), layer + 1, block))
        cast_out.append(pl.BlockSpec((rb, cols), block))
        cast_shapes.append(jax.ShapeDtypeStruct((rows, cols), BF16))
    return pl.pallas_call(
        functools.partial(_mlp_kernel, n_cast=len(next_weights)),
        grid=grid,
        in_specs=[
            pl.BlockSpec((MLP_TM, D_MODEL), lambda i, f: (i, 0)),
            _layer_spec((1, D_MODEL), layer, lambda i, f: (0, 0)),
            pl.BlockSpec((D_MODEL, MLP_TF), lambda i, f: (0, f)),
            pl.BlockSpec((MLP_TF, D_MODEL), lambda i, f: (f, 0)),
            _layer_spec((1, D_MODEL), layer, lambda i, f: (0, 0)),
        ] + cast_in,
        out_specs=[pl.BlockSpec((MLP_TM, D_MODEL), lambda i, f: (i, 0))] + cast_out,
        out_shape=[jax.ShapeDtypeStruct((t, D_MODEL), F32)] + cast_shapes,
        scratch_shapes=[pltpu.VMEM((MLP_TM, D_MODEL), BF16), pltpu.VMEM((MLP_TM, D_MODEL), F32)],
        compiler_params=_params(("parallel", "arbitrary"), 52),
        name="mlp",
    )(x, gpre, wu, wd, gpost, *next_weights)


def _scan_matrix():
    j = jnp.arange(LANES)[:, None]
    s = jnp.arange(LANES)[None, :]
    half = jnp.concatenate([(j >= s).astype(BF16), jnp.ones((LANES, LANES), BF16)], axis=1)
    return jnp.concatenate([half, half], axis=0)


def kernel(x, ln_mix_pre, ln_mix_post, ln_mlp_pre, ln_mlp_post, w_in, conv_a_w, proj_a, proj_b,
           conv_c_w, conv_c_b, norm_c_g, norm_c_b, proj_c, w_o, w_up, w_down):
    batch, seq, d = x.shape
    depth = w_in.shape[0]
    assert d == D_MODEL and w_in.shape[2] == N_IN
    assert seq % ATT_TQ == 0 and seq % CONV_TR == 0 and (batch * seq) % IN_TM == 0
    uo = _scan_matrix()
    ln_mix_pre, ln_mix_post, ln_mlp_pre, ln_mlp_post, conv_c_b, norm_c_g, norm_c_b = (
        p[:, None, :] for p in (ln_mix_pre, ln_mix_post, ln_mlp_pre, ln_mlp_post, conv_c_b, norm_c_g, norm_c_b))
    stacks = (proj_a, proj_b, proj_c, w_o, w_up, w_down)
    weights = [w[0].astype(BF16) for w in stacks]
    xt = x.reshape(batch * seq, d)
    for l in range(depth):
        pa, pb, pc, wo, wu, wd = weights
        proj = _in_proj(xt, ln_mix_pre, w_in, l)
        att = _attention(proj, uo, batch, seq)
        a_in, c_in = _conv_branches(proj, conv_a_w, conv_c_w, conv_c_b, norm_c_g, norm_c_b, l, batch, seq)
        xt = _merge(a_in, att, c_in, proj, xt, pa, pb, pc, wo, ln_mix_post, l)
        xt, *weights = _mlp(xt, ln_mlp_pre, wu, wd, ln_mlp_post, l, stacks if l + 1 < depth else ())
    return xt.reshape(batch, seq, d)
```

```python
import functools
import math

import jax
import jax.numpy as jnp
from jax import lax
from jax.experimental import pallas as pl
from jax.experimental.pallas import tpu as pltpu

F32 = jnp.float32
BF16 = jnp.bfloat16

D_MODEL = 2048
HEAD_DIM = 128
D_ATT = D_MODEL // 2
N_HEADS = D_ATT // HEAD_DIM
D_SC = D_MODEL // 4
SC_WIDTH = 3
D_CF = D_MODEL // 4
CF_WIDTH = 31
D_FF = 4 * D_MODEL
N_IN = 3 * D_ATT + 3 * D_SC + 2 * D_CF + 3 * D_MODEL
RMS_EPS = 1e-6
LN_EPS = 1e-5

SRC_SC = 3 * D_ATT
SRC_CF = SRC_SC + 3 * D_SC
SRC_GATE = SRC_CF + 2 * D_CF
COL_GATE = 0
COL_Q = 3 * D_MODEL
COL_K = COL_Q + D_ATT
COL_V = COL_K + D_ATT
COL_CF = COL_V + D_ATT
COL_SC = COL_CF + 2 * D_CF
assert COL_SC + 3 * D_SC == N_IN

LANES = 128
SUBLANES = 8
BF16_SUBLANES = 16
MIB = 1024 * 1024

NORM_CHUNK = 16
NORM_UNROLL = 4
IN_TM, IN_TN = 2048, 512
ATT_TQ = 256
ATT_HEADS_PER_STEP = 8
ATT_LOG2_SCALE = HEAD_DIM ** -0.5 * math.log2(math.e)
LOG2_E = math.log2(math.e)
MASKED_LOG2 = -1e30
CONV_TR = 256
CONV_HALO = 32
CONV_CHUNK = 32
CONV_SHIFT_ROWS = CONV_HALO + CONV_TR - SUBLANES
MERGE_TM = 256
MLP_TM, MLP_TF = 512, 1024


def _params(semantics, vmem_mib):
    return pltpu.CompilerParams(dimension_semantics=semantics,
                                vmem_limit_bytes=vmem_mib * MIB)


def _layer_spec(shape, layer, index_map):
    return pl.BlockSpec((None,) + shape, lambda *g: (layer,) + index_map(*g))


def _cast_specs(stacks, layer, n_steps, step_of):
    in_specs, out_specs, out_shapes = [], [], []
    for w in stacks:
        rows, cols = w.shape[1:]
        rb = max(rows // n_steps, BF16_SUBLANES)
        assert rows % rb == 0 and rows // rb <= n_steps
        last = rows // rb - 1

        def block(*g, last=last):
            return (jnp.minimum(step_of(*g), last), 0)

        in_specs.append(_layer_spec((rb, cols), layer, block))
        out_specs.append(pl.BlockSpec((rb, cols), block))
        out_shapes.append(jax.ShapeDtypeStruct((rows, cols), BF16))
    return in_specs, out_specs, out_shapes


def _cast_blocks(src_refs, dst_refs):
    for src_ref, dst_ref in zip(src_refs, dst_refs):
        dst_ref[...] = src_ref[...].astype(dst_ref.dtype)


def _rmsnorm_rows(x_ref, g_ref, h_ref):
    g = g_ref[...]

    def body(c, carry):
        r = pl.multiple_of(c * NORM_CHUNK, NORM_CHUNK)
        x = x_ref[pl.ds(r, NORM_CHUNK), :]
        ms = jnp.mean(x * x, axis=-1, keepdims=True)
        h_ref[pl.ds(r, NORM_CHUNK), :] = (x * lax.rsqrt(ms + RMS_EPS) * g).astype(h_ref.dtype)
        return carry

    lax.fori_loop(0, x_ref.shape[0] // NORM_CHUNK, body, 0, unroll=NORM_UNROLL)


def _in_proj_kernel(x_ref, g_ref, w_ref, o_ref, h_ref):
    @pl.when(pl.program_id(1) == 0)
    def _():
        _rmsnorm_rows(x_ref, g_ref, h_ref)

    w = w_ref[...].astype(BF16)
    o_ref[...] = jnp.dot(h_ref[...], w, preferred_element_type=F32).astype(o_ref.dtype)


def _in_proj_out_block(j):
    sc, cf, gate = SRC_SC // IN_TN, SRC_CF // IN_TN, SRC_GATE // IN_TN
    return jnp.where(j >= gate, j - gate + COL_GATE // IN_TN,
                     jnp.where(j < sc, j + COL_Q // IN_TN,
                               jnp.where(j < cf, j - sc + COL_SC // IN_TN, j - cf + COL_CF // IN_TN)))


def _in_proj(x, g, w, layer):
    t = x.shape[0]
    return pl.pallas_call(
        _in_proj_kernel,
        grid=(t // IN_TM, N_IN // IN_TN),
        in_specs=[
            pl.BlockSpec((IN_TM, D_MODEL), lambda i, j: (i, 0)),
            _layer_spec((1, D_MODEL), layer, lambda i, j: (0, 0)),
            _layer_spec((D_MODEL, IN_TN), layer, lambda i, j: (0, j)),
        ],
        out_specs=pl.BlockSpec((IN_TM, IN_TN), lambda i, j: (i, _in_proj_out_block(j))),
        out_shape=jax.ShapeDtypeStruct((t, N_IN), BF16),
        scratch_shapes=[pltpu.VMEM((IN_TM, D_MODEL), BF16)],
        compiler_params=_params(("parallel", "arbitrary"), 60),
        name="in_proj",
    )(x, g, w)


def _sb_scan(z2, uo):
    f = jnp.maximum(z2, 0.0) + jnp.log2(1.0 + jnp.exp2(-jnp.abs(z2)))
    hi = f.astype(BF16)
    lo = (f - hi.astype(F32)).astype(BF16)
    return [jnp.dot(jnp.concatenate([hi[:, c * LANES:(c + 1) * LANES], lo[:, c * LANES:(c + 1) * LANES]], axis=1),
                    uo, preferred_element_type=F32) for c in range(ATT_TQ // LANES)]


def _sb_weigh(z2, st, v, carry):
    n_sub = ATT_TQ // LANES
    a_parts = [None] * n_sub
    for c in reversed(range(n_sub)):
        suffix, total = st[c][:, :LANES], st[c][:, LANES:]
        if carry is not None:
            suffix = suffix + carry
        a_parts[c] = jnp.exp2(z2[:, c * LANES:(c + 1) * LANES] - suffix).astype(BF16)
        carry = total if carry is None else carry + total
    return carry, jnp.dot(jnp.concatenate(a_parts, axis=1), v, preferred_element_type=F32)


def _attn_kernel(q_ref, k_ref, v_ref, uo_ref, o_ref, carry_ref, acc_ref, pend_z_ref, pend_st_ref):
    qi = pl.program_id(2)
    uo = uo_ref[...]
    row = lax.broadcasted_iota(jnp.int32, (ATT_TQ, ATT_TQ), 0)
    col = lax.broadcasted_iota(jnp.int32, (ATT_TQ, ATT_TQ), 1)
    n = ATT_HEADS_PER_STEP
    n_sub = ATT_TQ // LANES
    heads = [slice(h * HEAD_DIM, (h + 1) * HEAD_DIM) for h in range(n)]
    qs = [q_ref[:, hs] for hs in heads]

    def rows(kb):
        return pl.ds(pl.multiple_of(kb * ATT_TQ, ATT_TQ), ATT_TQ)

    def weigh(h, z2, st, kb, start):
        carry, pv = _sb_weigh(z2, st, v_ref[rows(kb), heads[h]], None if start else carry_ref[h])
        carry_ref[h] = carry
        acc_ref[h] = pv if start else acc_ref[h] + pv

    def pair(kb, prev_kb, mask):
        z2s, sts = [None] * n, [None] * n
        for w in range(n):
            z2 = lax.dot_general(qs[w], k_ref[rows(kb), heads[w]], (((1,), (1,)), ((), ())),
                                 preferred_element_type=F32) * ATT_LOG2_SCALE
            z2s[w] = z2 if mask is None else jnp.where(mask, z2, MASKED_LOG2)
            if w >= 1:
                sts[w - 1] = _sb_scan(z2s[w - 1], uo)
            elif prev_kb is not None:
                pend_st_next = _sb_scan(pend_z_ref[1], uo)
            if w >= 2:
                weigh(w - 2, z2s[w - 2], sts[w - 2], kb, prev_kb is None)
            elif prev_kb is not None:
                st = [pend_st_ref[c] for c in range(n_sub)] if w == 0 else pend_st_next
                weigh(n - 2 + w, pend_z_ref[w], st, prev_kb, False)
        pend_z_ref[0] = z2s[n - 2]
        pend_z_ref[1] = z2s[n - 1]
        for c in range(n_sub):
            pend_st_ref[c] = sts[n - 2][c]

    for h in (n - 2, n - 1):
        carry_ref[h] = jnp.zeros((ATT_TQ, HEAD_DIM), F32)
        acc_ref[h] = jnp.zeros((ATT_TQ, HEAD_DIM), F32)
    pair(qi, None, col < row)

    def body(i, carry):
        kb = qi - 1 - i
        pair(kb, kb + 1, None)
        return carry

    lax.fori_loop(0, qi, body, 0)
    pend_st = _sb_scan(pend_z_ref[1], uo)
    weigh(n - 2, pend_z_ref[0], [pend_st_ref[c] for c in range(n_sub)], 0, False)
    weigh(n - 1, pend_z_ref[1], pend_st, 0, False)
    for h, hs in enumerate(heads):
        o_ref[:, hs] = acc_ref[h].astype(o_ref.dtype)


def _attention(proj, uo, batch, seq):
    nq = seq // ATT_TQ
    width = ATT_HEADS_PER_STEP * HEAD_DIM
    return pl.pallas_call(
        _attn_kernel,
        grid=(batch, N_HEADS // ATT_HEADS_PER_STEP, nq),
        in_specs=[
            pl.BlockSpec((ATT_TQ, width), lambda b, h, i: (b * nq + i, COL_Q // width + h)),
            pl.BlockSpec((seq, width), lambda b, h, i: (b, COL_K // width + h)),
            pl.BlockSpec((seq, width), lambda b, h, i: (b, COL_V // width + h)),
            pl.BlockSpec((2 * LANES, 2 * LANES), lambda b, h, i: (0, 0)),
        ],
        out_specs=pl.BlockSpec((ATT_TQ, width), lambda b, h, i: (b * nq + i, h)),
        out_shape=jax.ShapeDtypeStruct((batch * seq, D_ATT), BF16),
        scratch_shapes=[
            pltpu.VMEM((ATT_HEADS_PER_STEP, ATT_TQ, HEAD_DIM), F32),
            pltpu.VMEM((ATT_HEADS_PER_STEP, ATT_TQ, HEAD_DIM), F32),
            pltpu.VMEM((2, ATT_TQ, ATT_TQ), F32),
            pltpu.VMEM((ATT_TQ // LANES, ATT_TQ, 2 * LANES), F32),
        ],
        compiler_params=_params(("parallel", "parallel", "arbitrary"), 32),
        name="sb_attn",
    )(proj, proj, proj, uo)


def _conv_kernel(scb_ref, scc_ref, scu_ref, scc_h_ref, scu_h_ref, cf_ref, cf_h_ref,
                 wa_ref, wc_ref, bc_ref, lng_ref, lnb_ref, *refs, n_cast):
    src_refs, (a_out_ref, c_out_ref), dst_refs = refs[:n_cast], refs[n_cast:n_cast + 2], refs[n_cast + 2:-2]
    p_scr, u_scr = refs[-2:]
    first = pl.program_id(1) == 0
    _cast_blocks(src_refs, dst_refs)

    p_halo = scc_h_ref[...].astype(F32) * scu_h_ref[...].astype(F32)
    p_scr[:CONV_HALO, :] = jnp.where(first, 0.0, p_halo)
    p_scr[CONV_HALO:, :] = scc_ref[...].astype(F32) * scu_ref[...].astype(F32)

    def glu(ref):
        v = ref[...].astype(F32)
        return v[:, :D_CF] * jax.nn.sigmoid(v[:, D_CF:])

    u_scr[0, :CONV_HALO, :] = jnp.where(first, 0.0, glu(cf_h_ref))
    u_scr[0, CONV_HALO:, :] = glu(cf_ref)
    for b in range(1, SUBLANES):
        u_scr[b, :CONV_SHIFT_ROWS, :] = u_scr[0, b:b + CONV_SHIFT_ROWS, :]

    wa = wa_ref[...]
    wc = wc_ref[...]
    for c in range(CONV_TR // CONV_CHUNK):
        r0 = c * CONV_CHUNK
        rows = slice(r0, r0 + CONV_CHUNK)
        base = CONV_HALO - (SC_WIDTH - 1) + r0
        acc = wa[0:1, :] * p_scr[base:base + CONV_CHUNK, :]
        for k in range(1, SC_WIDTH):
            acc = acc + wa[k:k + 1, :] * p_scr[base + k:base + k + CONV_CHUNK, :]
        a_out_ref[rows, :] = (scb_ref[rows, :].astype(F32) * acc).astype(a_out_ref.dtype)
        base = CONV_HALO - (CF_WIDTH - 1) + r0
        acc = None
        for k in range(CF_WIDTH):
            shift, start = (base + k) % SUBLANES, (base + k) // SUBLANES * SUBLANES
            tap = wc[k:k + 1, :] * u_scr[shift, start:start + CONV_CHUNK, :]
            acc = tap if acc is None else acc + tap
        acc = acc + bc_ref[...]
        mu = jnp.mean(acc, axis=-1, keepdims=True)
        cen = acc - mu
        var = jnp.mean(cen * cen, axis=-1, keepdims=True)
        y = cen * lax.rsqrt(var + LN_EPS) * lng_ref[...] + lnb_ref[...]
        c_out_ref[rows, :] = (y * jax.nn.sigmoid(y)).astype(c_out_ref.dtype)


def _conv_branches(proj, wa, wc, bc, lng, lnb, layer, batch, seq, cast_stacks):
    nr = seq // CONV_TR
    hr = CONV_TR // CONV_HALO
    cast_in, cast_out, cast_shapes = _cast_specs(cast_stacks, layer, batch * nr, lambda b, i: b * nr + i)

    def cur(width, col):
        return pl.BlockSpec((CONV_TR, width), lambda b, i: (b * nr + i, col // width))

    def halo(width, col):
        return pl.BlockSpec((CONV_HALO, width),
                            lambda b, i: (jnp.maximum((b * nr + i) * hr - 1, 0), col // width))

    def whole(shape):
        return _layer_spec(shape, layer, lambda b, i: (0, 0))

    return pl.pallas_call(
        functools.partial(_conv_kernel, n_cast=len(cast_stacks)),
        grid=(batch, nr),
        in_specs=[
            cur(D_SC, COL_SC), cur(D_SC, COL_SC + D_SC), cur(D_SC, COL_SC + 2 * D_SC),
            halo(D_SC, COL_SC + D_SC), halo(D_SC, COL_SC + 2 * D_SC),
            cur(2 * D_CF, COL_CF), halo(2 * D_CF, COL_CF),
            whole((SC_WIDTH, D_SC)), whole((CF_WIDTH, D_CF)),
            whole((1, D_CF)), whole((1, D_CF)), whole((1, D_CF)),
        ] + cast_in,
        out_specs=[
            pl.BlockSpec((CONV_TR, D_SC), lambda b, i: (b * nr + i, 0)),
            pl.BlockSpec((CONV_TR, D_CF), lambda b, i: (b * nr + i, 0)),
        ] + cast_out,
        out_shape=[
            jax.ShapeDtypeStruct((batch * seq, D_SC), BF16),
            jax.ShapeDtypeStruct((batch * seq, D_CF), BF16),
        ] + cast_shapes,
        scratch_shapes=[
            pltpu.VMEM((CONV_HALO + CONV_TR, D_SC), F32),
            pltpu.VMEM((SUBLANES, CONV_HALO + CONV_TR, D_CF), F32),
        ],
        compiler_params=_params(("parallel", "arbitrary"), 40),
        name="conv_branches",
    )(proj, proj, proj, proj, proj, proj, proj, wa, wc, bc, lng, lnb, *cast_stacks)


def _merge_kernel(a_ref, att_ref, c_ref, ga_ref, gb_ref, gc_ref, x_ref,
                  pa_ref, pb_ref, pc_ref, wo_ref, g_ref, o_ref):
    def gated(gate_ref, feat_ref, w_ref):
        y = jnp.dot(feat_ref[...], w_ref[...], preferred_element_type=F32)
        return jax.nn.sigmoid(gate_ref[...].astype(F32)) * y

    mixed = gated(ga_ref, a_ref, pa_ref) + gated(gb_ref, att_ref, pb_ref) + gated(gc_ref, c_ref, pc_ref)
    m = jnp.dot(mixed.astype(BF16), wo_ref[...], preferred_element_type=F32)
    ms = jnp.mean(m * m, axis=-1, keepdims=True)
    o_ref[...] = x_ref[...] + m * lax.rsqrt(ms + RMS_EPS) * g_ref[...]


def _merge(a_in, att, c_in, proj, x, pa, pb, pc, wo, g, layer):
    t = x.shape[0]

    def rows(width, col=0):
        return pl.BlockSpec((MERGE_TM, width), lambda i: (i, col // width))

    def whole(shape):
        return pl.BlockSpec(shape, lambda i: (0, 0))

    return pl.pallas_call(
        _merge_kernel,
        grid=(t // MERGE_TM,),
        in_specs=[
            rows(D_SC), rows(D_ATT), rows(D_CF),
            rows(D_MODEL, COL_GATE), rows(D_MODEL, COL_GATE + D_MODEL), rows(D_MODEL, COL_GATE + 2 * D_MODEL),
            rows(D_MODEL),
            whole((D_SC, D_MODEL)), whole((D_ATT, D_MODEL)), whole((D_CF, D_MODEL)),
            whole((D_MODEL, D_MODEL)), _layer_spec((1, D_MODEL), layer, lambda i: (0, 0)),
        ],
        out_specs=rows(D_MODEL),
        out_shape=jax.ShapeDtypeStruct((t, D_MODEL), F32),
        compiler_params=_params(("parallel",), 56),
        name="merge",
    )(a_in, att, c_in, proj, proj, proj, x, pa, pb, pc, wo, g)


def _mlp_kernel(x_ref, gpre_ref, wu_ref, wd_ref, gpost_ref, *refs, n_cast):
    src_refs, o_ref, dst_refs, (h_ref, acc_ref) = refs[:n_cast], refs[n_cast], refs[n_cast + 1:-2], refs[-2:]
    f = pl.program_id(1)

    @pl.when(f == 0)
    def _():
        _rmsnorm_rows(x_ref, gpre_ref, h_ref)
        acc_ref[...] = jnp.zeros_like(acc_ref)

    u = jnp.dot(h_ref[...], wu_ref[...], preferred_element_type=F32)
    r = jnp.square(jnp.maximum(u, 0.0)).astype(BF16)
    acc_ref[...] += jnp.dot(r, wd_ref[...], preferred_element_type=F32)
    _cast_blocks(src_refs, dst_refs)

    @pl.when(f == pl.num_programs(1) - 1)
    def _():
        y = acc_ref[...]
        ms = jnp.mean(y * y, axis=-1, keepdims=True)
        o_ref[...] = x_ref[...] + y * lax.rsqrt(ms + RMS_EPS) * gpost_ref[...]


def _mlp(x, gpre, wu, wd, gpost, layer, next_weights):
    t = x.shape[0]
    grid = (t // MLP_TM, D_FF // MLP_TF)
    n_steps = grid[0] * grid[1]
    cast_in, cast_out, cast_shapes = _cast_specs(next_weights, layer + 1, n_steps, lambda i, f: i * grid[1] + f)
    return pl.pallas_call(
        functools.partial(_mlp_kernel, n_cast=len(next_weights)),
        grid=grid,
        in_specs=[
            pl.BlockSpec((MLP_TM, D_MODEL), lambda i, f: (i, 0)),
            _layer_spec((1, D_MODEL), layer, lambda i, f: (0, 0)),
            pl.BlockSpec((D_MODEL, MLP_TF), lambda i, f: (0, f)),
            pl.BlockSpec((MLP_TF, D_MODEL), lambda i, f: (f, 0)),
            _layer_spec((1, D_MODEL), layer, lambda i, f: (0, 0)),
        ] + cast_in,
        out_specs=[pl.BlockSpec((MLP_TM, D_MODEL), lambda i, f: (i, 0))] + cast_out,
        out_shape=[jax.ShapeDtypeStruct((t, D_MODEL), F32)] + cast_shapes,
        scratch_shapes=[pltpu.VMEM((MLP_TM, D_MODEL), BF16), pltpu.VMEM((MLP_TM, D_MODEL), F32)],
        compiler_params=_params(("parallel", "arbitrary"), 52),
        name="mlp",
    )(x, gpre, wu, wd, gpost, *next_weights)


def _scan_matrix():
    j = jnp.arange(LANES)[:, None]
    s = jnp.arange(LANES)[None, :]
    half = jnp.concatenate([(j >= s).astype(BF16), jnp.ones((LANES, LANES), BF16)], axis=1)
    return jnp.concatenate([half, half], axis=0)


def kernel(x, ln_mix_pre, ln_mix_post, ln_mlp_pre, ln_mlp_post, w_in, conv_a_w, proj_a, proj_b,
           conv_c_w, conv_c_b, norm_c_g, norm_c_b, proj_c, w_o, w_up, w_down):
    batch, seq, d = x.shape
    depth = w_in.shape[0]
    assert d == D_MODEL and w_in.shape[2] == N_IN
    assert seq % ATT_TQ == 0 and seq % CONV_TR == 0 and (batch * seq) % IN_TM == 0
    uo = _scan_matrix()
    ln_mix_pre, ln_mix_post, ln_mlp_pre, ln_mlp_post, conv_c_b, norm_c_g, norm_c_b = (
        p[:, None, :] for p in (ln_mix_pre, ln_mix_post, ln_mlp_pre, ln_mlp_post, conv_c_b, norm_c_g, norm_c_b))
    stacks = (proj_a, proj_b, proj_c, w_o, w_up, w_down)
    weights = []
    xt = x.reshape(batch * seq, d)
    for l in range(depth):
        proj = _in_proj(xt, ln_mix_pre, w_in, l)
        att = _attention(proj, uo, batch, seq)
        a_in, c_in, *cast = _conv_branches(proj, conv_a_w, conv_c_w, conv_c_b, norm_c_g, norm_c_b, l, batch, seq,
                                           stacks if l == 0 else ())
        pa, pb, pc, wo, wu, wd = cast if l == 0 else weights
        xt = _merge(a_in, att, c_in, proj, xt, pa, pb, pc, wo, ln_mix_post, l)
        xt, *weights = _mlp(xt, ln_mlp_pre, wu, wd, ln_mlp_post, l, stacks if l + 1 < depth else ())
    return xt.reshape(batch, seq, d)
```

```python
import functools
import math

import jax
import jax.numpy as jnp
from jax import lax
from jax.experimental import pallas as pl
from jax.experimental.pallas import tpu as pltpu

F32 = jnp.float32
BF16 = jnp.bfloat16

D_MODEL = 2048
HEAD_DIM = 128
D_ATT = D_MODEL // 2
N_HEADS = D_ATT // HEAD_DIM
D_SC = D_MODEL // 4
SC_WIDTH = 3
D_CF = D_MODEL // 4
CF_WIDTH = 31
D_FF = 4 * D_MODEL
N_IN = 3 * D_ATT + 3 * D_SC + 2 * D_CF + 3 * D_MODEL
RMS_EPS = 1e-6
LN_EPS = 1e-5

SRC_SC = 3 * D_ATT
SRC_CF = SRC_SC + 3 * D_SC
SRC_GATE = SRC_CF + 2 * D_CF
COL_GATE = 0
COL_Q = 3 * D_MODEL
COL_K = COL_Q + D_ATT
COL_V = COL_K + D_ATT
COL_CF = COL_V + D_ATT
COL_SC = COL_CF + 2 * D_CF
assert COL_SC + 3 * D_SC == N_IN

LANES = 128
SUBLANES = 8
BF16_SUBLANES = 16
MIB = 1024 * 1024

NORM_CHUNK = 16
NORM_UNROLL = 16
IN_TM, IN_TN = 2048, 512
ATT_TQ = 256
ATT_HEADS_PER_STEP = 8
ATT_LOG2_SCALE = HEAD_DIM ** -0.5 * math.log2(math.e)
LOG2_E = math.log2(math.e)
MASKED_LOG2 = -1e30
CONV_TR = 256
CONV_HALO = 32
CONV_CHUNK = 32
CONV_SHIFT_ROWS = CONV_HALO + CONV_TR - SUBLANES
MERGE_TM = 256
MLP_TM, MLP_TF = 512, 1024


def _params(semantics, vmem_mib):
    return pltpu.CompilerParams(dimension_semantics=semantics,
                                vmem_limit_bytes=vmem_mib * MIB)


def _layer_spec(shape, layer, index_map):
    return pl.BlockSpec((None,) + shape, lambda *g: (layer,) + index_map(*g))


def _cast_specs(stacks, layer, n_steps, step_of):
    in_specs, out_specs, out_shapes = [], [], []
    for w in stacks:
        rows, cols = w.shape[1:]
        rb = max(rows // n_steps, BF16_SUBLANES)
        assert rows % rb == 0 and rows // rb <= n_steps
        last = rows // rb - 1

        def block(*g, last=last):
            return (jnp.minimum(step_of(*g), last), 0)

        in_specs.append(_layer_spec((rb, cols), layer, block))
        out_specs.append(pl.BlockSpec((rb, cols), block))
        out_shapes.append(jax.ShapeDtypeStruct((rows, cols), BF16))
    return in_specs, out_specs, out_shapes


def _cast_blocks(src_refs, dst_refs):
    for src_ref, dst_ref in zip(src_refs, dst_refs):
        dst_ref[...] = src_ref[...].astype(dst_ref.dtype)


def _rmsnorm_rows(x_ref, g_ref, h_ref, zero_ref=None):
    g = g_ref[...]

    def body(c, carry):
        r = pl.multiple_of(c * NORM_CHUNK, NORM_CHUNK)
        x = x_ref[pl.ds(r, NORM_CHUNK), :]
        ms = jnp.mean(x * x, axis=-1, keepdims=True)
        h_ref[pl.ds(r, NORM_CHUNK), :] = (x * lax.rsqrt(ms + RMS_EPS) * g).astype(h_ref.dtype)
        if zero_ref is not None:
            zero_ref[pl.ds(r, NORM_CHUNK), :] = jnp.zeros((NORM_CHUNK, zero_ref.shape[1]), zero_ref.dtype)
        return carry

    lax.fori_loop(0, x_ref.shape[0] // NORM_CHUNK, body, 0, unroll=NORM_UNROLL)


def _in_proj_kernel(x_ref, g_ref, w_ref, o_ref, h_ref):
    @pl.when(pl.program_id(1) == 0)
    def _():
        _rmsnorm_rows(x_ref, g_ref, h_ref)

    w = w_ref[...].astype(BF16)
    o_ref[...] = jnp.dot(h_ref[...], w, preferred_element_type=F32).astype(o_ref.dtype)


def _in_proj_out_block(j):
    sc, cf, gate = SRC_SC // IN_TN, SRC_CF // IN_TN, SRC_GATE // IN_TN
    return jnp.where(j >= gate, j - gate + COL_GATE // IN_TN,
                     jnp.where(j < sc, j + COL_Q // IN_TN,
                               jnp.where(j < cf, j - sc + COL_SC // IN_TN, j - cf + COL_CF // IN_TN)))


def _in_proj(x, g, w, layer):
    t = x.shape[0]
    return pl.pallas_call(
        _in_proj_kernel,
        grid=(t // IN_TM, N_IN // IN_TN),
        in_specs=[
            pl.BlockSpec((IN_TM, D_MODEL), lambda i, j: (i, 0)),
            _layer_spec((1, D_MODEL), layer, lambda i, j: (0, 0)),
            _layer_spec((D_MODEL, IN_TN), layer, lambda i, j: (0, j)),
        ],
        out_specs=pl.BlockSpec((IN_TM, IN_TN), lambda i, j: (i, _in_proj_out_block(j))),
        out_shape=jax.ShapeDtypeStruct((t, N_IN), BF16),
        scratch_shapes=[pltpu.VMEM((IN_TM, D_MODEL), BF16)],
        compiler_params=_params(("parallel", "arbitrary"), 60),
        name="in_proj",
    )(x, g, w)


def _sb_scan(z2, uo):
    f = jnp.maximum(z2, 0.0) + jnp.log2(1.0 + jnp.exp2(-jnp.abs(z2)))
    hi = f.astype(BF16)
    lo = (f - hi.astype(F32)).astype(BF16)
    return [jnp.dot(jnp.concatenate([hi[:, c * LANES:(c + 1) * LANES], lo[:, c * LANES:(c + 1) * LANES]], axis=1),
                    uo, preferred_element_type=F32) for c in range(ATT_TQ // LANES)]


def _sb_weigh(z2, st, v, carry):
    n_sub = ATT_TQ // LANES
    a_parts = [None] * n_sub
    for c in reversed(range(n_sub)):
        suffix, total = st[c][:, :LANES], st[c][:, LANES:]
        if carry is not None:
            suffix = suffix + carry
        a_parts[c] = jnp.exp2(z2[:, c * LANES:(c + 1) * LANES] - suffix).astype(BF16)
        carry = total if carry is None else carry + total
    return carry, jnp.dot(jnp.concatenate(a_parts, axis=1), v, preferred_element_type=F32)


def _attn_kernel(q_ref, k_ref, v_ref, uo_ref, o_ref, carry_ref, acc_ref, pend_z_ref, pend_st_ref):
    qi = pl.program_id(2)
    uo = uo_ref[...]
    row = lax.broadcasted_iota(jnp.int32, (ATT_TQ, ATT_TQ), 0)
    col = lax.broadcasted_iota(jnp.int32, (ATT_TQ, ATT_TQ), 1)
    n = ATT_HEADS_PER_STEP
    n_sub = ATT_TQ // LANES
    heads = [slice(h * HEAD_DIM, (h + 1) * HEAD_DIM) for h in range(n)]
    qs = [q_ref[:, hs] for hs in heads]

    def rows(kb):
        return pl.ds(pl.multiple_of(kb * ATT_TQ, ATT_TQ), ATT_TQ)

    def weigh(h, z2, st, kb, start):
        carry, pv = _sb_weigh(z2, st, v_ref[rows(kb), heads[h]], None if start else carry_ref[h])
        carry_ref[h] = carry
        acc_ref[h] = pv if start else acc_ref[h] + pv

    def pair(kb, prev_kb, mask):
        z2s, sts = [None] * n, [None] * n
        for w in range(n):
            z2 = lax.dot_general(qs[w], k_ref[rows(kb), heads[w]], (((1,), (1,)), ((), ())),
                                 preferred_element_type=F32) * ATT_LOG2_SCALE
            z2s[w] = z2 if mask is None else jnp.where(mask, z2, MASKED_LOG2)
            if w >= 1:
                sts[w - 1] = _sb_scan(z2s[w - 1], uo)
            elif prev_kb is not None:
                pend_st_next = _sb_scan(pend_z_ref[1], uo)
            if w >= 2:
                weigh(w - 2, z2s[w - 2], sts[w - 2], kb, prev_kb is None)
            elif prev_kb is not None:
                st = [pend_st_ref[c] for c in range(n_sub)] if w == 0 else pend_st_next
                weigh(n - 2 + w, pend_z_ref[w], st, prev_kb, False)
        pend_z_ref[0] = z2s[n - 2]
        pend_z_ref[1] = z2s[n - 1]
        for c in range(n_sub):
            pend_st_ref[c] = sts[n - 2][c]

    for h in (n - 2, n - 1):
        carry_ref[h] = jnp.zeros((ATT_TQ, HEAD_DIM), F32)
        acc_ref[h] = jnp.zeros((ATT_TQ, HEAD_DIM), F32)
    pair(qi, None, col < row)

    def body(i, carry):
        kb = qi - 1 - i
        pair(kb, kb + 1, None)
        return carry

    lax.fori_loop(0, qi, body, 0)
    pend_st = _sb_scan(pend_z_ref[1], uo)
    weigh(n - 2, pend_z_ref[0], [pend_st_ref[c] for c in range(n_sub)], 0, False)
    weigh(n - 1, pend_z_ref[1], pend_st, 0, False)
    for h, hs in enumerate(heads):
        o_ref[:, hs] = acc_ref[h].astype(o_ref.dtype)


def _attention(proj, uo, batch, seq):
    nq = seq // ATT_TQ
    width = ATT_HEADS_PER_STEP * HEAD_DIM
    return pl.pallas_call(
        _attn_kernel,
        grid=(batch, N_HEADS // ATT_HEADS_PER_STEP, nq),
        in_specs=[
            pl.BlockSpec((ATT_TQ, width), lambda b, h, i: (b * nq + i, COL_Q // width + h)),
            pl.BlockSpec((seq, width), lambda b, h, i: (b, COL_K // width + h)),
            pl.BlockSpec((seq, width), lambda b, h, i: (b, COL_V // width + h)),
            pl.BlockSpec((2 * LANES, 2 * LANES), lambda b, h, i: (0, 0)),
        ],
        out_specs=pl.BlockSpec((ATT_TQ, width), lambda b, h, i: (b * nq + i, h)),
        out_shape=jax.ShapeDtypeStruct((batch * seq, D_ATT), BF16),
        scratch_shapes=[
            pltpu.VMEM((ATT_HEADS_PER_STEP, ATT_TQ, HEAD_DIM), F32),
            pltpu.VMEM((ATT_HEADS_PER_STEP, ATT_TQ, HEAD_DIM), F32),
            pltpu.VMEM((2, ATT_TQ, ATT_TQ), F32),
            pltpu.VMEM((ATT_TQ // LANES, ATT_TQ, 2 * LANES), F32),
        ],
        compiler_params=_params(("parallel", "parallel", "arbitrary"), 32),
        name="sb_attn",
    )(proj, proj, proj, uo)


def _conv_kernel(scb_ref, scc_ref, scu_ref, scc_h_ref, scu_h_ref, cf_ref, cf_h_ref,
                 wa_ref, wc_ref, bc_ref, lng_ref, lnb_ref, *refs, n_cast):
    src_refs, (a_out_ref, c_out_ref), dst_refs = refs[:n_cast], refs[n_cast:n_cast + 2], refs[n_cast + 2:-2]
    p_scr, u_scr = refs[-2:]
    first = pl.program_id(1) == 0
    _cast_blocks(src_refs, dst_refs)

    p_halo = scc_h_ref[...].astype(F32) * scu_h_ref[...].astype(F32)
    p_scr[:CONV_HALO, :] = jnp.where(first, 0.0, p_halo)
    p_scr[CONV_HALO:, :] = scc_ref[...].astype(F32) * scu_ref[...].astype(F32)

    def glu(ref):
        v = ref[...].astype(F32)
        return v[:, :D_CF] * jax.nn.sigmoid(v[:, D_CF:])

    u_scr[0, :CONV_HALO, :] = jnp.where(first, 0.0, glu(cf_h_ref))
    u_scr[0, CONV_HALO:, :] = glu(cf_ref)
    for b in range(1, SUBLANES):
        u_scr[b, :CONV_SHIFT_ROWS, :] = u_scr[0, b:b + CONV_SHIFT_ROWS, :]

    wa = wa_ref[...]
    wc = wc_ref[...]
    for c in range(CONV_TR // CONV_CHUNK):
        r0 = c * CONV_CHUNK
        rows = slice(r0, r0 + CONV_CHUNK)
        base = CONV_HALO - (SC_WIDTH - 1) + r0
        acc = wa[0:1, :] * p_scr[base:base + CONV_CHUNK, :]
        for k in range(1, SC_WIDTH):
            acc = acc + wa[k:k + 1, :] * p_scr[base + k:base + k + CONV_CHUNK, :]
        a_out_ref[rows, :] = (scb_ref[rows, :].astype(F32) * acc).astype(a_out_ref.dtype)
        base = CONV_HALO - (CF_WIDTH - 1) + r0
        acc = None
        for k in range(CF_WIDTH):
            shift, start = (base + k) % SUBLANES, (base + k) // SUBLANES * SUBLANES
            tap = wc[k:k + 1, :] * u_scr[shift, start:start + CONV_CHUNK, :]
            acc = tap if acc is None else acc + tap
        acc = acc + bc_ref[...]
        mu = jnp.mean(acc, axis=-1, keepdims=True)
        cen = acc - mu
        var = jnp.mean(cen * cen, axis=-1, keepdims=True)
        y = cen * lax.rsqrt(var + LN_EPS) * lng_ref[...] + lnb_ref[...]
        c_out_ref[rows, :] = (y * jax.nn.sigmoid(y)).astype(c_out_ref.dtype)


def _conv_branches(proj, wa, wc, bc, lng, lnb, layer, batch, seq, cast_stacks):
    nr = seq // CONV_TR
    hr = CONV_TR // CONV_HALO
    cast_in, cast_out, cast_shapes = _cast_specs(cast_stacks, layer, batch * nr, lambda b, i: b * nr + i)

    def cur(width, col):
        return pl.BlockSpec((CONV_TR, width), lambda b, i: (b * nr + i, col // width))

    def halo(width, col):
        return pl.BlockSpec((CONV_HALO, width),
                            lambda b, i: (jnp.maximum((b * nr + i) * hr - 1, 0), col // width))

    def whole(shape):
        return _layer_spec(shape, layer, lambda b, i: (0, 0))

    return pl.pallas_call(
        functools.partial(_conv_kernel, n_cast=len(cast_stacks)),
        grid=(batch, nr),
        in_specs=[
            cur(D_SC, COL_SC), cur(D_SC, COL_SC + D_SC), cur(D_SC, COL_SC + 2 * D_SC),
            halo(D_SC, COL_SC + D_SC), halo(D_SC, COL_SC + 2 * D_SC),
            cur(2 * D_CF, COL_CF), halo(2 * D_CF, COL_CF),
            whole((SC_WIDTH, D_SC)), whole((CF_WIDTH, D_CF)),
            whole((1, D_CF)), whole((1, D_CF)), whole((1, D_CF)),
        ] + cast_in,
        out_specs=[
            pl.BlockSpec((CONV_TR, D_SC), lambda b, i: (b * nr + i, 0)),
            pl.BlockSpec((CONV_TR, D_CF), lambda b, i: (b * nr + i, 0)),
        ] + cast_out,
        out_shape=[
            jax.ShapeDtypeStruct((batch * seq, D_SC), BF16),
            jax.ShapeDtypeStruct((batch * seq, D_CF), BF16),
        ] + cast_shapes,
        scratch_shapes=[
            pltpu.VMEM((CONV_HALO + CONV_TR, D_SC), F32),
            pltpu.VMEM((SUBLANES, CONV_HALO + CONV_TR, D_CF), F32),
        ],
        compiler_params=_params(("parallel", "arbitrary"), 40),
        name="conv_branches",
    )(proj, proj, proj, proj, proj, proj, proj, wa, wc, bc, lng, lnb, *cast_stacks)


def _merge_kernel(a_ref, att_ref, c_ref, ga_ref, gb_ref, gc_ref, x_ref,
                  pa_ref, pb_ref, pc_ref, wo_ref, g_ref, o_ref):
    def gated(gate_ref, feat_ref, w_ref):
        y = jnp.dot(feat_ref[...], w_ref[...], preferred_element_type=F32)
        return jax.nn.sigmoid(gate_ref[...].astype(F32)) * y

    mixed = gated(ga_ref, a_ref, pa_ref) + gated(gb_ref, att_ref, pb_ref) + gated(gc_ref, c_ref, pc_ref)
    m = jnp.dot(mixed.astype(BF16), wo_ref[...], preferred_element_type=F32)
    ms = jnp.mean(m * m, axis=-1, keepdims=True)
    o_ref[...] = x_ref[...] + m * lax.rsqrt(ms + RMS_EPS) * g_ref[...]


def _merge(a_in, att, c_in, proj, x, pa, pb, pc, wo, g, layer):
    t = x.shape[0]

    def rows(width, col=0):
        return pl.BlockSpec((MERGE_TM, width), lambda i: (i, col // width))

    def whole(shape):
        return pl.BlockSpec(shape, lambda i: (0, 0))

    return pl.pallas_call(
        _merge_kernel,
        grid=(t // MERGE_TM,),
        in_specs=[
            rows(D_SC), rows(D_ATT), rows(D_CF),
            rows(D_MODEL, COL_GATE), rows(D_MODEL, COL_GATE + D_MODEL), rows(D_MODEL, COL_GATE + 2 * D_MODEL),
            rows(D_MODEL),
            whole((D_SC, D_MODEL)), whole((D_ATT, D_MODEL)), whole((D_CF, D_MODEL)),
            whole((D_MODEL, D_MODEL)), _layer_spec((1, D_MODEL), layer, lambda i: (0, 0)),
        ],
        out_specs=rows(D_MODEL),
        out_shape=jax.ShapeDtypeStruct((t, D_MODEL), F32),
        compiler_params=_params(("parallel",), 56),
        name="merge",
    )(a_in, att, c_in, proj, proj, proj, x, pa, pb, pc, wo, g)


def _mlp_kernel(x_ref, gpre_ref, wu_ref, wd_ref, gpost_ref, *refs, n_cast):
    src_refs, o_ref, dst_refs, (h_ref, acc_ref) = refs[:n_cast], refs[n_cast], refs[n_cast + 1:-2], refs[-2:]
    f = pl.program_id(1)

    @pl.when(f == 0)
    def _():
        _rmsnorm_rows(x_ref, gpre_ref, h_ref, zero_ref=acc_ref)

    u = jnp.dot(h_ref[...], wu_ref[...], preferred_element_type=F32)
    r = jnp.square(jnp.maximum(u, 0.0)).astype(BF16)
    acc_ref[...] += jnp.dot(r, wd_ref[...], preferred_element_type=F32)
    _cast_blocks(src_refs, dst_refs)

    @pl.when(f == pl.num_programs(1) - 1)
    def _():
        g = gpost_ref[...]

        def body(c, carry):
            rows = pl.ds(pl.multiple_of(c * NORM_CHUNK, NORM_CHUNK), NORM_CHUNK)
            y = acc_ref[rows, :]
            ms = jnp.mean(y * y, axis=-1, keepdims=True)
            o_ref[rows, :] = x_ref[rows, :] + y * lax.rsqrt(ms + RMS_EPS) * g
            return carry

        lax.fori_loop(0, acc_ref.shape[0] // NORM_CHUNK, body, 0, unroll=NORM_UNROLL)


def _mlp(x, gpre, wu, wd, gpost, layer, next_weights):
    t = x.shape[0]
    grid = (t // MLP_TM, D_FF // MLP_TF)
    n_steps = grid[0] * grid[1]
    cast_in, cast_out, cast_shapes = _cast_specs(next_weights, layer + 1, n_steps, lambda i, f: i * grid[1] + f)
    return pl.pallas_call(
        functools.partial(_mlp_kernel, n_cast=len(next_weights)),
        grid=grid,
        in_specs=[
            pl.BlockSpec((MLP_TM, D_MODEL), lambda i, f: (i, 0)),
            _layer_spec((1, D_MODEL), layer, lambda i, f: (0, 0)),
            pl.BlockSpec((D_MODEL, MLP_TF), lambda i, f: (0, f)),
            pl.BlockSpec((MLP_TF, D_MODEL), lambda i, f: (f, 0)),
            _layer_spec((1, D_MODEL), layer, lambda i, f: (0, 0)),
        ] + cast_in,
        out_specs=[pl.BlockSpec((MLP_TM, D_MODEL), lambda i, f: (i, 0))] + cast_out,
        out_shape=[jax.ShapeDtypeStruct((t, D_MODEL), F32)] + cast_shapes,
        scratch_shapes=[pltpu.VMEM((MLP_TM, D_MODEL), BF16), pltpu.VMEM((MLP_TM, D_MODEL), F32)],
        compiler_params=_params(("parallel", "arbitrary"), 52),
        name="mlp",
    )(x, gpre, wu, wd, gpost, *next_weights)


def _scan_matrix():
    j = jnp.arange(LANES)[:, None]
    s = jnp.arange(LANES)[None, :]
    half = jnp.concatenate([(j >= s).astype(BF16), jnp.ones((LANES, LANES), BF16)], axis=1)
    return jnp.concatenate([half, half], axis=0)


def kernel(x, ln_mix_pre, ln_mix_post, ln_mlp_pre, ln_mlp_post, w_in, conv_a_w, proj_a, proj_b,
           conv_c_w, conv_c_b, norm_c_g, norm_c_b, proj_c, w_o, w_up, w_down):
    batch, seq, d = x.shape
    depth = w_in.shape[0]
    assert d == D_MODEL and w_in.shape[2] == N_IN
    assert seq % ATT_TQ == 0 and seq % CONV_TR == 0 and (batch * seq) % IN_TM == 0
    uo = _scan_matrix()
    ln_mix_pre, ln_mix_post, ln_mlp_pre, ln_mlp_post, conv_c_b, norm_c_g, norm_c_b = (
        p[:, None, :] for p in (ln_mix_pre, ln_mix_post, ln_mlp_pre, ln_mlp_post, conv_c_b, norm_c_g, norm_c_b))
    stacks = (proj_a, proj_b, proj_c, w_o, w_up, w_down)
    weights = []
    xt = x.reshape(batch * seq, d)
    for l in range(depth):
        proj = _in_proj(xt, ln_mix_pre, w_in, l)
        att = _attention(proj, uo, batch, seq)
        a_in, c_in, *cast = _conv_branches(proj, conv_a_w, conv_c_w, conv_c_b, norm_c_g, norm_c_b, l, batch, seq,
                                           stacks if l == 0 else ())
        pa, pb, pc, wo, wu, wd = cast if l == 0 else weights
        xt = _merge(a_in, att, c_in, proj, xt, pa, pb, pc, wo, ln_mix_post, l)
        xt, *weights = _mlp(xt, ln_mlp_pre, wu, wd, ln_mlp_post, l, stacks if l + 1 < depth else ())
    return xt.reshape(batch, seq, d)
```

```python
import functools
import math

import jax
import jax.numpy as jnp
from jax import lax
from jax.experimental import pallas as pl
from jax.experimental.pallas import tpu as pltpu

F32 = jnp.float32
BF16 = jnp.bfloat16

D_MODEL = 2048
HEAD_DIM = 128
D_ATT = D_MODEL // 2
N_HEADS = D_ATT // HEAD_DIM
D_SC = D_MODEL // 4
SC_WIDTH = 3
D_CF = D_MODEL // 4
CF_WIDTH = 31
D_FF = 4 * D_MODEL
N_IN = 3 * D_ATT + 3 * D_SC + 2 * D_CF + 3 * D_MODEL
RMS_EPS = 1e-6
LN_EPS = 1e-5

SRC_SC = 3 * D_ATT
SRC_CF = SRC_SC + 3 * D_SC
SRC_GATE = SRC_CF + 2 * D_CF
COL_GATE = 0
COL_Q = 3 * D_MODEL
COL_K = COL_Q + D_ATT
COL_V = COL_K + D_ATT
COL_CF = COL_V + D_ATT
COL_SC = COL_CF + 2 * D_CF
assert COL_SC + 3 * D_SC == N_IN

LANES = 128
SUBLANES = 8
BF16_SUBLANES = 16
MIB = 1024 * 1024

NORM_CHUNK = 16
NORM_UNROLL = 16
IN_TM, IN_TN = 2048, 512
ATT_TQ = 256
ATT_HEADS_PER_STEP = 8
ATT_LOG2_SCALE = HEAD_DIM ** -0.5 * math.log2(math.e)
LOG2_E = math.log2(math.e)
MASKED_LOG2 = -1e30
CONV_TR = 256
CONV_HALO = 32
CONV_CHUNK = 32
CONV_SHIFT_ROWS = CONV_HALO + CONV_TR - SUBLANES
MERGE_TM = 512
MLP_TM, MLP_TF = 512, 1024


def _params(semantics, vmem_mib):
    return pltpu.CompilerParams(dimension_semantics=semantics,
                                vmem_limit_bytes=vmem_mib * MIB)


def _layer_spec(shape, layer, index_map):
    return pl.BlockSpec((None,) + shape, lambda *g: (layer,) + index_map(*g))


def _cast_specs(stacks, layer, n_steps, step_of):
    in_specs, out_specs, out_shapes = [], [], []
    for w in stacks:
        rows, cols = w.shape[1:]
        rb = max(rows // n_steps, BF16_SUBLANES)
        assert rows % rb == 0 and rows // rb <= n_steps
        last = rows // rb - 1

        def block(*g, last=last):
            return (jnp.minimum(step_of(*g), last), 0)

        in_specs.append(_layer_spec((rb, cols), layer, block))
        out_specs.append(pl.BlockSpec((rb, cols), block))
        out_shapes.append(jax.ShapeDtypeStruct((rows, cols), BF16))
    return in_specs, out_specs, out_shapes


def _cast_blocks(src_refs, dst_refs):
    for src_ref, dst_ref in zip(src_refs, dst_refs):
        dst_ref[...] = src_ref[...].astype(dst_ref.dtype)


def _rmsnorm_rows(x_ref, g_ref, h_ref, zero_ref=None):
    g = g_ref[...]

    def body(c, carry):
        r = pl.multiple_of(c * NORM_CHUNK, NORM_CHUNK)
        x = x_ref[pl.ds(r, NORM_CHUNK), :]
        ms = jnp.mean(x * x, axis=-1, keepdims=True)
        h_ref[pl.ds(r, NORM_CHUNK), :] = (x * lax.rsqrt(ms + RMS_EPS) * g).astype(h_ref.dtype)
        if zero_ref is not None:
            zero_ref[pl.ds(r, NORM_CHUNK), :] = jnp.zeros((NORM_CHUNK, zero_ref.shape[1]), zero_ref.dtype)
        return carry

    lax.fori_loop(0, x_ref.shape[0] // NORM_CHUNK, body, 0, unroll=NORM_UNROLL)


def _in_proj_kernel(x_ref, g_ref, w_ref, o_ref, h_ref):
    @pl.when(pl.program_id(1) == 0)
    def _():
        _rmsnorm_rows(x_ref, g_ref, h_ref)

    w = w_ref[...].astype(BF16)
    o_ref[...] = jnp.dot(h_ref[...], w, preferred_element_type=F32).astype(o_ref.dtype)


def _in_proj_out_block(j):
    sc, cf, gate = SRC_SC // IN_TN, SRC_CF // IN_TN, SRC_GATE // IN_TN
    return jnp.where(j >= gate, j - gate + COL_GATE // IN_TN,
                     jnp.where(j < sc, j + COL_Q // IN_TN,
                               jnp.where(j < cf, j - sc + COL_SC // IN_TN, j - cf + COL_CF // IN_TN)))


def _in_proj(x, g, w, layer):
    t = x.shape[0]
    return pl.pallas_call(
        _in_proj_kernel,
        grid=(t // IN_TM, N_IN // IN_TN),
        in_specs=[
            pl.BlockSpec((IN_TM, D_MODEL), lambda i, j: (i, 0)),
            _layer_spec((1, D_MODEL), layer, lambda i, j: (0, 0)),
            _layer_spec((D_MODEL, IN_TN), layer, lambda i, j: (0, j)),
        ],
        out_specs=pl.BlockSpec((IN_TM, IN_TN), lambda i, j: (i, _in_proj_out_block(j))),
        out_shape=jax.ShapeDtypeStruct((t, N_IN), BF16),
        scratch_shapes=[pltpu.VMEM((IN_TM, D_MODEL), BF16)],
        compiler_params=_params(("parallel", "arbitrary"), 60),
        name="in_proj",
    )(x, g, w)


def _sb_scan(z2, uo):
    f = jnp.maximum(z2, 0.0) + jnp.log2(1.0 + jnp.exp2(-jnp.abs(z2)))
    hi = f.astype(BF16)
    lo = (f - hi.astype(F32)).astype(BF16)
    return [jnp.dot(jnp.concatenate([hi[:, c * LANES:(c + 1) * LANES], lo[:, c * LANES:(c + 1) * LANES]], axis=1),
                    uo, preferred_element_type=F32) for c in range(ATT_TQ // LANES)]


def _sb_weigh(z2, st, v, carry):
    n_sub = ATT_TQ // LANES
    a_parts = [None] * n_sub
    for c in reversed(range(n_sub)):
        suffix, total = st[c][:, :LANES], st[c][:, LANES:]
        if carry is not None:
            suffix = suffix + carry
        a_parts[c] = jnp.exp2(z2[:, c * LANES:(c + 1) * LANES] - suffix).astype(BF16)
        carry = total if carry is None else carry + total
    return carry, jnp.dot(jnp.concatenate(a_parts, axis=1), v, preferred_element_type=F32)


def _attn_kernel(q_ref, k_ref, v_ref, uo_ref, o_ref, carry_ref, acc_ref, pend_z_ref, pend_st_ref):
    qi = pl.program_id(2)
    uo = uo_ref[...]
    row = lax.broadcasted_iota(jnp.int32, (ATT_TQ, ATT_TQ), 0)
    col = lax.broadcasted_iota(jnp.int32, (ATT_TQ, ATT_TQ), 1)
    n = ATT_HEADS_PER_STEP
    n_sub = ATT_TQ // LANES
    heads = [slice(h * HEAD_DIM, (h + 1) * HEAD_DIM) for h in range(n)]
    qs = [q_ref[:, hs] for hs in heads]

    def rows(kb):
        return pl.ds(pl.multiple_of(kb * ATT_TQ, ATT_TQ), ATT_TQ)

    def weigh(h, z2, st, kb, start):
        carry, pv = _sb_weigh(z2, st, v_ref[rows(kb), heads[h]], None if start else carry_ref[h])
        carry_ref[h] = carry
        acc_ref[h] = pv if start else acc_ref[h] + pv

    def pair(kb, prev_kb, mask):
        z2s, sts = [None] * n, [None] * n
        for w in range(n):
            z2 = lax.dot_general(qs[w], k_ref[rows(kb), heads[w]], (((1,), (1,)), ((), ())),
                                 preferred_element_type=F32) * ATT_LOG2_SCALE
            z2s[w] = z2 if mask is None else jnp.where(mask, z2, MASKED_LOG2)
            if w >= 1:
                sts[w - 1] = _sb_scan(z2s[w - 1], uo)
            elif prev_kb is not None:
                pend_st_next = _sb_scan(pend_z_ref[1], uo)
            if w >= 2:
                weigh(w - 2, z2s[w - 2], sts[w - 2], kb, prev_kb is None)
            elif prev_kb is not None:
                st = [pend_st_ref[c] for c in range(n_sub)] if w == 0 else pend_st_next
                weigh(n - 2 + w, pend_z_ref[w], st, prev_kb, False)
        pend_z_ref[0] = z2s[n - 2]
        pend_z_ref[1] = z2s[n - 1]
        for c in range(n_sub):
            pend_st_ref[c] = sts[n - 2][c]

    for h in (n - 2, n - 1):
        carry_ref[h] = jnp.zeros((ATT_TQ, HEAD_DIM), F32)
        acc_ref[h] = jnp.zeros((ATT_TQ, HEAD_DIM), F32)
    pair(qi, None, col < row)

    def body(i, carry):
        kb = qi - 1 - i
        pair(kb, kb + 1, None)
        return carry

    lax.fori_loop(0, qi, body, 0)
    pend_st = _sb_scan(pend_z_ref[1], uo)
    weigh(n - 2, pend_z_ref[0], [pend_st_ref[c] for c in range(n_sub)], 0, False)
    weigh(n - 1, pend_z_ref[1], pend_st, 0, False)
    for h, hs in enumerate(heads):
        o_ref[:, hs] = acc_ref[h].astype(o_ref.dtype)


def _attention(proj, uo, batch, seq):
    nq = seq // ATT_TQ
    width = ATT_HEADS_PER_STEP * HEAD_DIM
    return pl.pallas_call(
        _attn_kernel,
        grid=(batch, N_HEADS // ATT_HEADS_PER_STEP, nq),
        in_specs=[
            pl.BlockSpec((ATT_TQ, width), lambda b, h, i: (b * nq + i, COL_Q // width + h)),
            pl.BlockSpec((seq, width), lambda b, h, i: (b, COL_K // width + h)),
            pl.BlockSpec((seq, width), lambda b, h, i: (b, COL_V // width + h)),
            pl.BlockSpec((2 * LANES, 2 * LANES), lambda b, h, i: (0, 0)),
        ],
        out_specs=pl.BlockSpec((ATT_TQ, width), lambda b, h, i: (b * nq + i, h)),
        out_shape=jax.ShapeDtypeStruct((batch * seq, D_ATT), BF16),
        scratch_shapes=[
            pltpu.VMEM((ATT_HEADS_PER_STEP, ATT_TQ, HEAD_DIM), F32),
            pltpu.VMEM((ATT_HEADS_PER_STEP, ATT_TQ, HEAD_DIM), F32),
            pltpu.VMEM((2, ATT_TQ, ATT_TQ), F32),
            pltpu.VMEM((ATT_TQ // LANES, ATT_TQ, 2 * LANES), F32),
        ],
        compiler_params=_params(("parallel", "parallel", "arbitrary"), 32),
        name="sb_attn",
    )(proj, proj, proj, uo)


def _conv_kernel(scb_ref, scc_ref, scu_ref, scc_h_ref, scu_h_ref, cf_ref, cf_h_ref,
                 wa_ref, wc_ref, bc_ref, lng_ref, lnb_ref, *refs, n_cast):
    src_refs, (a_out_ref, c_out_ref), dst_refs = refs[:n_cast], refs[n_cast:n_cast + 2], refs[n_cast + 2:-2]
    p_scr, u_scr = refs[-2:]
    first = pl.program_id(1) == 0
    _cast_blocks(src_refs, dst_refs)

    p_halo = scc_h_ref[...].astype(F32) * scu_h_ref[...].astype(F32)
    p_scr[:CONV_HALO, :] = jnp.where(first, 0.0, p_halo)
    p_scr[CONV_HALO:, :] = scc_ref[...].astype(F32) * scu_ref[...].astype(F32)

    def glu(ref):
        v = ref[...].astype(F32)
        return v[:, :D_CF] * jax.nn.sigmoid(v[:, D_CF:])

    u_scr[0, :CONV_HALO, :] = jnp.where(first, 0.0, glu(cf_h_ref))
    u_scr[0, CONV_HALO:, :] = glu(cf_ref)
    for b in range(1, SUBLANES):
        u_scr[b, :CONV_SHIFT_ROWS, :] = u_scr[0, b:b + CONV_SHIFT_ROWS, :]

    wa = wa_ref[...]
    wc = wc_ref[...]
    for c in range(CONV_TR // CONV_CHUNK):
        r0 = c * CONV_CHUNK
        rows = slice(r0, r0 + CONV_CHUNK)
        base = CONV_HALO - (SC_WIDTH - 1) + r0
        acc = wa[0:1, :] * p_scr[base:base + CONV_CHUNK, :]
        for k in range(1, SC_WIDTH):
            acc = acc + wa[k:k + 1, :] * p_scr[base + k:base + k + CONV_CHUNK, :]
        a_out_ref[rows, :] = (scb_ref[rows, :].astype(F32) * acc).astype(a_out_ref.dtype)
        base = CONV_HALO - (CF_WIDTH - 1) + r0
        acc = None
        for k in range(CF_WIDTH):
            shift, start = (base + k) % SUBLANES, (base + k) // SUBLANES * SUBLANES
            tap = wc[k:k + 1, :] * u_scr[shift, start:start + CONV_CHUNK, :]
            acc = tap if acc is None else acc + tap
        acc = acc + bc_ref[...]
        mu = jnp.mean(acc, axis=-1, keepdims=True)
        cen = acc - mu
        var = jnp.mean(cen * cen, axis=-1, keepdims=True)
        y = cen * lax.rsqrt(var + LN_EPS) * lng_ref[...] + lnb_ref[...]
        c_out_ref[rows, :] = (y * jax.nn.sigmoid(y)).astype(c_out_ref.dtype)


def _conv_branches(proj, wa, wc, bc, lng, lnb, layer, batch, seq, cast_stacks):
    nr = seq // CONV_TR
    hr = CONV_TR // CONV_HALO
    cast_in, cast_out, cast_shapes = _cast_specs(cast_stacks, layer, batch * nr, lambda b, i: b * nr + i)

    def cur(width, col):
        return pl.BlockSpec((CONV_TR, width), lambda b, i: (b * nr + i, col // width))

    def halo(width, col):
        return pl.BlockSpec((CONV_HALO, width),
                            lambda b, i: (jnp.maximum((b * nr + i) * hr - 1, 0), col // width))

    def whole(shape):
        return _layer_spec(shape, layer, lambda b, i: (0, 0))

    return pl.pallas_call(
        functools.partial(_conv_kernel, n_cast=len(cast_stacks)),
        grid=(batch, nr),
        in_specs=[
            cur(D_SC, COL_SC), cur(D_SC, COL_SC + D_SC), cur(D_SC, COL_SC + 2 * D_SC),
            halo(D_SC, COL_SC + D_SC), halo(D_SC, COL_SC + 2 * D_SC),
            cur(2 * D_CF, COL_CF), halo(2 * D_CF, COL_CF),
            whole((SC_WIDTH, D_SC)), whole((CF_WIDTH, D_CF)),
            whole((1, D_CF)), whole((1, D_CF)), whole((1, D_CF)),
        ] + cast_in,
        out_specs=[
            pl.BlockSpec((CONV_TR, D_SC), lambda b, i: (b * nr + i, 0)),
            pl.BlockSpec((CONV_TR, D_CF), lambda b, i: (b * nr + i, 0)),
        ] + cast_out,
        out_shape=[
            jax.ShapeDtypeStruct((batch * seq, D_SC), BF16),
            jax.ShapeDtypeStruct((batch * seq, D_CF), BF16),
        ] + cast_shapes,
        scratch_shapes=[
            pltpu.VMEM((CONV_HALO + CONV_TR, D_SC), F32),
            pltpu.VMEM((SUBLANES, CONV_HALO + CONV_TR, D_CF), F32),
        ],
        compiler_params=_params(("parallel", "arbitrary"), 40),
        name="conv_branches",
    )(proj, proj, proj, proj, proj, proj, proj, wa, wc, bc, lng, lnb, *cast_stacks)


def _merge_kernel(a_ref, att_ref, c_ref, ga_ref, gb_ref, gc_ref, x_ref,
                  pa_ref, pb_ref, pc_ref, wo_ref, g_ref, o_ref):
    def gated(gate_ref, feat_ref, w_ref):
        y = jnp.dot(feat_ref[...], w_ref[...], preferred_element_type=F32)
        return jax.nn.sigmoid(gate_ref[...].astype(F32)) * y

    mixed = gated(ga_ref, a_ref, pa_ref) + gated(gb_ref, att_ref, pb_ref) + gated(gc_ref, c_ref, pc_ref)
    m = jnp.dot(mixed.astype(BF16), wo_ref[...], preferred_element_type=F32)
    ms = jnp.mean(m * m, axis=-1, keepdims=True)
    o_ref[...] = x_ref[...] + m * lax.rsqrt(ms + RMS_EPS) * g_ref[...]


def _merge(a_in, att, c_in, proj, x, pa, pb, pc, wo, g, layer):
    t = x.shape[0]

    def rows(width, col=0):
        return pl.BlockSpec((MERGE_TM, width), lambda i: (i, col // width))

    def whole(shape):
        return pl.BlockSpec(shape, lambda i: (0, 0), pipeline_mode=pl.Buffered(1))

    return pl.pallas_call(
        _merge_kernel,
        grid=(t // MERGE_TM,),
        in_specs=[
            rows(D_SC), rows(D_ATT), rows(D_CF),
            rows(D_MODEL, COL_GATE), rows(D_MODEL, COL_GATE + D_MODEL), rows(D_MODEL, COL_GATE + 2 * D_MODEL),
            rows(D_MODEL),
            whole((D_SC, D_MODEL)), whole((D_ATT, D_MODEL)), whole((D_CF, D_MODEL)),
            whole((D_MODEL, D_MODEL)), _layer_spec((1, D_MODEL), layer, lambda i: (0, 0)),
        ],
        out_specs=rows(D_MODEL),
        out_shape=jax.ShapeDtypeStruct((t, D_MODEL), F32),
        compiler_params=_params(("parallel",), 56),
        name="merge",
    )(a_in, att, c_in, proj, proj, proj, x, pa, pb, pc, wo, g)


def _mlp_kernel(x_ref, gpre_ref, wu_ref, wd_ref, gpost_ref, *refs, n_cast):
    src_refs, o_ref, dst_refs, (h_ref, acc_ref) = refs[:n_cast], refs[n_cast], refs[n_cast + 1:-2], refs[-2:]
    f = pl.program_id(1)

    @pl.when(f == 0)
    def _():
        _rmsnorm_rows(x_ref, gpre_ref, h_ref, zero_ref=acc_ref)

    u = jnp.dot(h_ref[...], wu_ref[...], preferred_element_type=F32)
    r = jnp.square(jnp.maximum(u, 0.0)).astype(BF16)
    acc_ref[...] += jnp.dot(r, wd_ref[...], preferred_element_type=F32)
    _cast_blocks(src_refs, dst_refs)

    @pl.when(f == pl.num_programs(1) - 1)
    def _():
        g = gpost_ref[...]

        def body(c, carry):
            rows = pl.ds(pl.multiple_of(c * NORM_CHUNK, NORM_CHUNK), NORM_CHUNK)
            y = acc_ref[rows, :]
            ms = jnp.mean(y * y, axis=-1, keepdims=True)
            o_ref[rows, :] = x_ref[rows, :] + y * lax.rsqrt(ms + RMS_EPS) * g
            return carry

        lax.fori_loop(0, acc_ref.shape[0] // NORM_CHUNK, body, 0, unroll=NORM_UNROLL)


def _mlp(x, gpre, wu, wd, gpost, layer, next_weights):
    t = x.shape[0]
    grid = (t // MLP_TM, D_FF // MLP_TF)
    n_steps = grid[0] * grid[1]
    cast_in, cast_out, cast_shapes = _cast_specs(next_weights, layer + 1, n_steps, lambda i, f: i * grid[1] + f)
    return pl.pallas_call(
        functools.partial(_mlp_kernel, n_cast=len(next_weights)),
        grid=grid,
        in_specs=[
            pl.BlockSpec((MLP_TM, D_MODEL), lambda i, f: (i, 0)),
            _layer_spec((1, D_MODEL), layer, lambda i, f: (0, 0)),
            pl.BlockSpec((D_MODEL, MLP_TF), lambda i, f: (0, f)),
            pl.BlockSpec((MLP_TF, D_MODEL), lambda i, f: (f, 0)),
            _layer_spec((1, D_MODEL), layer, lambda i, f: (0, 0)),
        ] + cast_in,
        out_specs=[pl.BlockSpec((MLP_TM, D_MODEL), lambda i, f: (i, 0))] + cast_out,
        out_shape=[jax.ShapeDtypeStruct((t, D_MODEL), F32)] + cast_shapes,
        scratch_shapes=[pltpu.VMEM((MLP_TM, D_MODEL), BF16), pltpu.VMEM((MLP_TM, D_MODEL), F32)],
        compiler_params=_params(("parallel", "arbitrary"), 52),
        name="mlp",
    )(x, gpre, wu, wd, gpost, *next_weights)


def _scan_matrix():
    j = jnp.arange(LANES)[:, None]
    s = jnp.arange(LANES)[None, :]
    half = jnp.concatenate([(j >= s).astype(BF16), jnp.ones((LANES, LANES), BF16)], axis=1)
    return jnp.concatenate([half, half], axis=0)


def kernel(x, ln_mix_pre, ln_mix_post, ln_mlp_pre, ln_mlp_post, w_in, conv_a_w, proj_a, proj_b,
           conv_c_w, conv_c_b, norm_c_g, norm_c_b, proj_c, w_o, w_up, w_down):
    batch, seq, d = x.shape
    depth = w_in.shape[0]
    assert d == D_MODEL and w_in.shape[2] == N_IN
    assert seq % ATT_TQ == 0 and seq % CONV_TR == 0 and (batch * seq) % IN_TM == 0
    uo = _scan_matrix()
    ln_mix_pre, ln_mix_post, ln_mlp_pre, ln_mlp_post, conv_c_b, norm_c_g, norm_c_b = (
        p[:, None, :] for p in (ln_mix_pre, ln_mix_post, ln_mlp_pre, ln_mlp_post, conv_c_b, norm_c_g, norm_c_b))
    stacks = (proj_a, proj_b, proj_c, w_o, w_up, w_down)
    weights = []
    xt = x.reshape(batch * seq, d)
    for l in range(depth):
        proj = _in_proj(xt, ln_mix_pre, w_in, l)
        att = _attention(proj, uo, batch, seq)
        a_in, c_in, *cast = _conv_branches(proj, conv_a_w, conv_c_w, conv_c_b, norm_c_g, norm_c_b, l, batch, seq,
                                           stacks if l == 0 else ())
        pa, pb, pc, wo, wu, wd = cast if l == 0 else weights
        xt = _merge(a_in, att, c_in, proj, xt, pa, pb, pc, wo, ln_mix_post, l)
        xt, *weights = _mlp(xt, ln_mlp_pre, wu, wd, ln_mlp_post, l, stacks if l + 1 < depth else ())
    return xt.reshape(batch, seq, d)
```
